```python
import math
import jax, jax.numpy as jnp
from jax import lax
import numpy as np

D_MODEL = 1024
BATCH = 8
SEQ = 2048
DEPTH = 4
DEC_BATCH = 128
DEC_SEQ = 8
PAST_LEN = 2048
PAGE_SIZE = 128

N_MIXERS = 3
N_SB = (DEPTH + 2) // 3
N_SC = (DEPTH + 1) // 3
N_RG = DEPTH // 3
SB_HEADS = 16
SB_HEAD_DIM = D_MODEL // SB_HEADS
SB_BIAS_INIT = -6.0
Q_BLOCK = 128
SC_WIDTH = D_MODEL
SC_KERNEL = 3
RG_WIDTH = D_MODEL
RG_BLOCKS = 8
RG_BLOCK_DIM = RG_WIDTH // RG_BLOCKS
RG_KERNEL = 4
RG_C = 8.0
PEER_HEADS = 8
PEER_KEY_DIM = 256
PEER_HALF = PEER_KEY_DIM // 2
N_KEYS = 128
N_EXPERTS = N_KEYS * N_KEYS
PEER_TOPK = 16
PEER_CHUNK = 128
EPS = 1e-6

kernel_name = 'hybrid_sb_conv_rglru_peer_step'


def rmsnorm(x, g):
    xf = x.astype(jnp.float32)
    y = xf * lax.rsqrt(jnp.mean(xf * xf, axis=-1, keepdims=True) + EPS)
    return (y * g.astype(jnp.float32)).astype(x.dtype)


def causal_depthwise_conv(u, buf, w):
    width = w.shape[0]
    t = u.shape[1]
    ext = jnp.concatenate([buf.astype(u.dtype), u], axis=1)
    y = ext[:, 0:t] * w[0]
    for j in range(1, width):
        y = y + ext[:, j:j + t] * w[j]
    return y, ext[:, t:]


def stick_breaking(q, k, v, q_pos, k_pos, bias):
    z = jnp.einsum('bqhd,bkhd->bhqk', q, k, preferred_element_type=jnp.float32) / math.sqrt(SB_HEAD_DIM)
    z = z + bias.astype(jnp.float32)[None, :, None, None]
    mask = k_pos[None, :] < q_pos[:, None]
    log_not = jnp.where(mask, -jax.nn.softplus(z), 0.0)
    tail = lax.cumsum(log_not, axis=3, reverse=True) - log_not
    w = jnp.where(mask, jnp.exp(-jax.nn.softplus(-z) + tail), 0.0)
    return jnp.einsum('bhqk,bkhd->bqhd', w.astype(v.dtype), v)


def sb_qkv(h, w_qkv):
    b, t, _ = h.shape
    qkv = (h @ w_qkv).reshape(b, t, 3, SB_HEADS, SB_HEAD_DIM)
    return qkv[:, :, 0], qkv[:, :, 1], qkv[:, :, 2]


def sb_prompt(h, w_qkv, w_o, bias):
    b, t, _ = h.shape
    q, k, v = sb_qkv(h, w_qkv)
    n_blk = t // Q_BLOCK
    qb = jnp.moveaxis(q.reshape(b, n_blk, Q_BLOCK, SB_HEADS, SB_HEAD_DIM), 1, 0)
    k_pos = jnp.arange(t)

    def one_block(args):
        q_blk, i = args
        q_pos = i * Q_BLOCK + jnp.arange(Q_BLOCK)
        return stick_breaking(q_blk, k, v, q_pos, k_pos, bias)

    o = lax.map(one_block, (qb, jnp.arange(n_blk)))
    o = jnp.moveaxis(o, 0, 1).reshape(b, t, D_MODEL)
    return o @ w_o, k, v


def sb_sample(h, cache_k, cache_v, page_table, layer, w_qkv, w_o, bias):
    b, t, _ = h.shape
    q, k, v = sb_qkv(h, w_qkv)
    past = page_table.shape[1] * PAGE_SIZE
    k_past = cache_k[layer][page_table].reshape(b, past, SB_HEADS, SB_HEAD_DIM).astype(k.dtype)
    v_past = cache_v[layer][page_table].reshape(b, past, SB_HEADS, SB_HEAD_DIM).astype(v.dtype)
    k_all = jnp.concatenate([k_past, k], axis=1)
    v_all = jnp.concatenate([v_past, v], axis=1)
    q_pos = past + jnp.arange(t)
    k_pos = jnp.arange(past + t)
    o = stick_breaking(q, k_all, v_all, q_pos, k_pos, bias).reshape(b, t, D_MODEL)
    return o @ w_o, k, v


def short_conv(h, buf, w_in, conv_w, w_out):
    b_gate, c_gate, xv = jnp.split(h @ w_in, 3, axis=-1)
    y, new_buf = causal_depthwise_conv(c_gate * xv, buf, conv_w)
    return (b_gate * y) @ w_out, new_buf


def lru_combine(e1, e2):
    a1, b1 = e1
    a2, b2 = e2
    return a1 * a2, a2 * b1 + b2


def rglru_block(h, conv_buf, h0, w_in, conv_w, conv_b, w_ra, b_ra, w_ri, b_ri, lam, w_out):
    b, t, _ = h.shape
    gate, u = jnp.split(h @ w_in, 2, axis=-1)
    u, new_buf = causal_depthwise_conv(u, conv_buf, conv_w)
    u = u + conv_b
    ub = u.reshape(b, t, RG_BLOCKS, RG_BLOCK_DIM)
    r = jax.nn.sigmoid(jnp.einsum('btnc,ncd->btnd', ub, w_ra).reshape(b, t, RG_WIDTH) + b_ra)
    ig = jax.nn.sigmoid(jnp.einsum('btnc,ncd->btnd', ub, w_ri).reshape(b, t, RG_WIDTH) + b_ri)
    log_a = -RG_C * r.astype(jnp.float32) * jax.nn.softplus(-lam.astype(jnp.float32))
    a = jnp.exp(log_a)
    drive = jnp.sqrt(-jnp.expm1(2.0 * log_a)) * (ig * u).astype(jnp.float32)
    a_cum, b_cum = lax.associative_scan(lru_combine, (a, drive), axis=1)
    hs = a_cum * h0.astype(jnp.float32)[:, None, :] + b_cum
    out = (jax.nn.gelu(gate, approximate=False) * hs.astype(h.dtype)) @ w_out
    return out, new_buf, hs[:, -1].astype(h0.dtype)


def peer_ffn(x, w_query, keys1, keys2, u_table, v_table):
    shp = x.shape
    xt = x.reshape(-1, D_MODEL)
    n = xt.shape[0]
    n_pad = (-n) % PEER_CHUNK
    xc = jnp.pad(xt, ((0, n_pad), (0, 0))).reshape(-1, PEER_CHUNK, D_MODEL)

    def chunk(xb):
        q = (xb @ w_query).reshape(PEER_CHUNK, PEER_HEADS, PEER_KEY_DIM).astype(jnp.float32)
        s1 = jnp.einsum('thc,hnc->thn', q[..., :PEER_HALF], keys1.astype(jnp.float32))
        s2 = jnp.einsum('thc,hnc->thn', q[..., PEER_HALF:], keys2.astype(jnp.float32))
        v1, i1 = lax.top_k(s1, PEER_TOPK)
        v2, i2 = lax.top_k(s2, PEER_TOPK)
        cand = (v1[..., :, None] + v2[..., None, :]).reshape(PEER_CHUNK, PEER_HEADS, PEER_TOPK * PEER_TOPK)
        sc, ci = lax.top_k(cand, PEER_TOPK)
        e1 = jnp.take_along_axis(i1, ci // PEER_TOPK, axis=-1)
        e2 = jnp.take_along_axis(i2, ci % PEER_TOPK, axis=-1)
        ids = e1 * N_KEYS + e2
        g = jax.nn.softmax(sc, axis=-1)
        act = jax.nn.gelu(jnp.einsum('thkd,td->thk', u_table[ids], xb).astype(jnp.float32), approximate=False)
        return jnp.einsum('thk,thkd->td', (g * act).astype(xb.dtype), v_table[ids])

    y = lax.map(chunk, xc).reshape(-1, D_MODEL)[:n]
    return y.reshape(shp)


def setup_inputs(seed: int = 0) -> dict:
    key = jax.random.key(seed)
    ks = iter(jax.random.split(key, 40))

    def nrm(shape, scale):
        return jax.random.normal(next(ks), shape, jnp.float32) * scale

    n_pages = PAST_LEN // PAGE_SIZE
    n_pool = (5 * DEC_BATCH * n_pages) // 4
    page_table = jax.random.permutation(next(ks), n_pool)[: DEC_BATCH * n_pages].reshape(DEC_BATCH, n_pages).astype(jnp.int32)
    a0 = jax.random.uniform(next(ks), (N_RG, RG_WIDTH), jnp.float32, minval=0.9, maxval=0.999)
    dinv = D_MODEL ** -0.5
    return {
        'x_prompt': nrm((BATCH, SEQ, D_MODEL), 1.0),
        'x_sample': nrm((DEC_BATCH, DEC_SEQ, D_MODEL), 1.0),
        'cache_k_sb': nrm((N_SB, n_pool, PAGE_SIZE, SB_HEADS, SB_HEAD_DIM), 1.0),
        'cache_v_sb': nrm((N_SB, n_pool, PAGE_SIZE, SB_HEADS, SB_HEAD_DIM), 1.0),
        'page_table': page_table,
        'state_conv_sc': nrm((N_SC, DEC_BATCH, SC_KERNEL - 1, SC_WIDTH), 1.0),
        'state_conv_rg': nrm((N_RG, DEC_BATCH, RG_KERNEL - 1, RG_WIDTH), 1.0),
        'state_h_rg': nrm((N_RG, DEC_BATCH, RG_WIDTH), 1.0),
        'norm_mix': 1.0 + nrm((DEPTH, D_MODEL), 0.02),
        'norm_ffn': 1.0 + nrm((DEPTH, D_MODEL), 0.02),
        'norm_final': 1.0 + nrm((D_MODEL,), 0.02),
        'w_qkv_sb': nrm((N_SB, D_MODEL, 3 * D_MODEL), dinv),
        'w_o_sb': nrm((N_SB, D_MODEL, D_MODEL), dinv),
        'b_sb': SB_BIAS_INIT + nrm((N_SB, SB_HEADS), 0.1),
        'w_in_sc': nrm((N_SC, D_MODEL, 3 * SC_WIDTH), dinv),
        'conv_w_sc': nrm((N_SC, SC_KERNEL, SC_WIDTH), SC_KERNEL ** -0.5),
        'w_out_sc': nrm((N_SC, SC_WIDTH, D_MODEL), SC_WIDTH ** -0.5),
        'w_in_rg': nrm((N_RG, D_MODEL, 2 * RG_WIDTH), dinv),
        'conv_w_rg': nrm((N_RG, RG_KERNEL, RG_WIDTH), RG_KERNEL ** -0.5),
        'conv_b_rg': nrm((N_RG, RG_WIDTH), 0.02),
        'w_ra_rg': nrm((N_RG, RG_BLOCKS, RG_BLOCK_DIM, RG_BLOCK_DIM), RG_BLOCK_DIM ** -0.5),
        'b_ra_rg': nrm((N_RG, RG_WIDTH), 0.02),
        'w_ri_rg': nrm((N_RG, RG_BLOCKS, RG_BLOCK_DIM, RG_BLOCK_DIM), RG_BLOCK_DIM ** -0.5),
        'b_ri_rg': nrm((N_RG, RG_WIDTH), 0.02),
        'lam_rg': jnp.log(a0) - jnp.log1p(-a0),
        'w_out_rg': nrm((N_RG, RG_WIDTH, D_MODEL), RG_WIDTH ** -0.5),
        'w_query_peer': nrm((DEPTH, D_MODEL, PEER_HEADS * PEER_KEY_DIM), dinv),
        'keys1_peer': nrm((DEPTH, PEER_HEADS, N_KEYS, PEER_HALF), PEER_HALF ** -0.5),
        'keys2_peer': nrm((DEPTH, PEER_HEADS, N_KEYS, PEER_HALF), PEER_HALF ** -0.5),
        'u_peer': nrm((DEPTH, N_EXPERTS, D_MODEL), dinv),
        'v_peer': nrm((DEPTH, N_EXPERTS, D_MODEL), dinv),
    }


def reference(x_prompt, x_sample, cache_k_sb, cache_v_sb, page_table, state_conv_sc, state_conv_rg,
              state_h_rg, norm_mix, norm_ffn, norm_final, w_qkv_sb, w_o_sb, b_sb, w_in_sc, conv_w_sc,
              w_out_sc, w_in_rg, conv_w_rg, conv_b_rg, w_ra_rg, b_ra_rg, w_ri_rg, b_ri_rg, lam_rg,
              w_out_rg, w_query_peer, keys1_peer, keys2_peer, u_peer, v_peer):
    xp, xs = x_prompt, x_sample
    bp = xp.shape[0]
    k_sb_p, v_sb_p, k_sb_s, v_sb_s = [], [], [], []
    conv_sc_p, conv_sc_s = [], []
    conv_rg_p, conv_rg_s, h_rg_p, h_rg_s = [], [], [], []
    for layer in range(DEPTH):
        kind, j = layer % N_MIXERS, layer // N_MIXERS
        hp = rmsnorm(xp, norm_mix[layer])
        hs = rmsnorm(xs, norm_mix[layer])
        if kind == 0:
            op, kp, vp = sb_prompt(hp, w_qkv_sb[j], w_o_sb[j], b_sb[j])
            os_, kss, vss = sb_sample(hs, cache_k_sb, cache_v_sb, page_table, j, w_qkv_sb[j], w_o_sb[j], b_sb[j])
            k_sb_p.append(kp)
            v_sb_p.append(vp)
            k_sb_s.append(kss)
            v_sb_s.append(vss)
        elif kind == 1:
            buf0 = jnp.zeros((bp, SC_KERNEL - 1, SC_WIDTH), xp.dtype)
            op, bufp = short_conv(hp, buf0, w_in_sc[j], conv_w_sc[j], w_out_sc[j])
            os_, bufs = short_conv(hs, state_conv_sc[j], w_in_sc[j], conv_w_sc[j], w_out_sc[j])
            conv_sc_p.append(bufp)
            conv_sc_s.append(bufs)
        else:
            buf0 = jnp.zeros((bp, RG_KERNEL - 1, RG_WIDTH), xp.dtype)
            h0 = jnp.zeros((bp, RG_WIDTH), xp.dtype)
            rg_w = (w_in_rg[j], conv_w_rg[j], conv_b_rg[j], w_ra_rg[j], b_ra_rg[j], w_ri_rg[j], b_ri_rg[j], lam_rg[j], w_out_rg[j])
            op, bufp, hlp = rglru_block(hp, buf0, h0, *rg_w)
            os_, bufs, hls = rglru_block(hs, state_conv_rg[j], state_h_rg[j], *rg_w)
            conv_rg_p.append(bufp)
            conv_rg_s.append(bufs)
            h_rg_p.append(hlp)
            h_rg_s.append(hls)
        xp = xp + op
        xs = xs + os_
        peer_w = (w_query_peer[layer], keys1_peer[layer], keys2_peer[layer], u_peer[layer], v_peer[layer])
        xp = xp + peer_ffn(rmsnorm(xp, norm_ffn[layer]), *peer_w)
        xs = xs + peer_ffn(rmsnorm(xs, norm_ffn[layer]), *peer_w)
    y_prompt = rmsnorm(xp, norm_final)
    y_sample = rmsnorm(xs, norm_final)
    return (y_prompt, y_sample,
            jnp.stack(k_sb_p), jnp.stack(v_sb_p), jnp.stack(conv_sc_p), jnp.stack(conv_rg_p), jnp.stack(h_rg_p),
            jnp.stack(k_sb_s), jnp.stack(v_sb_s), jnp.stack(conv_sc_s), jnp.stack(conv_rg_s), jnp.stack(h_rg_s))
```

```python
import functools
import math

import jax
import jax.numpy as jnp
from jax import lax
from jax.experimental import pallas as pl
from jax.experimental.pallas import tpu as pltpu

F32 = jnp.float32
BF16 = jnp.bfloat16
EPS = 1e-6
HEAD_DIM = 64
N_HEADS = 16
PAGE = 128
Q_TILE = 128
RG_BLOCK = 128
RG_C = 8.0
TOPK = 16
N_KEYS = 128
PEER_HEADS = 8
LANES = 128
SUBLANES = 8
VMEM_LIMIT = 48 * 1024 * 1024

NT_DIMS = (((1,), (1,)), ((), ()))
TN_DIMS = (((0,), (0,)), ((), ()))


def _params(*sem):
    return pltpu.CompilerParams(dimension_semantics=sem, vmem_limit_bytes=VMEM_LIMIT)


def _softplus(z):
    return jnp.maximum(z, 0.0) + jnp.log(1.0 + jnp.exp(-jnp.abs(z)))


def _sigmoid(z):
    return 1.0 / (1.0 + jnp.exp(-z))


def _gelu(x):
    return 0.5 * x * (1.0 + lax.erf(x * (1.0 / math.sqrt(2.0))))


def _rms(x, g):
    ms = jnp.mean(x * x, axis=-1, keepdims=True)
    return x * lax.rsqrt(ms + EPS) * g


def _split_bf16(x):
    hi = x.astype(BF16)
    lo = (x - hi.astype(F32)).astype(BF16)
    return hi, lo


def _norm_mm_kernel(x_ref, g_ref, w_ref, o_ref, hn_ref):
    @pl.when(pl.program_id(1) == 0)
    def _():
        hn_ref[...] = _rms(x_ref[...], g_ref[...]).astype(BF16)

    o_ref[...] = jnp.dot(hn_ref[...], w_ref[...], preferred_element_type=F32)


def norm_matmul(x, g, w, tm=512, tn=1024):
    t, d = x.shape
    n = w.shape[1]
    tm = min(tm, t)
    tn = min(tn, n)
    return pl.pallas_call(
        _norm_mm_kernel,
        out_shape=jax.ShapeDtypeStruct((t, n), F32),
        grid=(t // tm, n // tn),
        in_specs=[pl.BlockSpec((tm, d), lambda i, j: (i, 0)),
                  pl.BlockSpec((1, d), lambda i, j: (0, 0)),
                  pl.BlockSpec((d, tn), lambda i, j: (0, j))],
        out_specs=pl.BlockSpec((tm, tn), lambda i, j: (i, j)),
        scratch_shapes=[pltpu.VMEM((tm, d), BF16)],
        compiler_params=_params("parallel", "arbitrary"),
        name="norm_matmul",
    )(x, g, w)


def _mm_res_kernel(a_ref, w_ref, r_ref, o_ref):
    o_ref[...] = r_ref[...] + jnp.dot(a_ref[...].astype(BF16), w_ref[...], preferred_element_type=F32)


def matmul_residual(a, w, res, tm=512):
    t, k = a.shape
    n = w.shape[1]
    tm = min(tm, t)
    return pl.pallas_call(
        _mm_res_kernel,
        out_shape=jax.ShapeDtypeStruct((t, n), F32),
        grid=(t // tm,),
        in_specs=[pl.BlockSpec((tm, k), lambda i: (i, 0)),
                  pl.BlockSpec((k, n), lambda i: (0, 0)),
                  pl.BlockSpec((tm, n), lambda i: (i, 0))],
        out_specs=pl.BlockSpec((tm, n), lambda i: (i, 0)),
        compiler_params=_params("parallel"),
        name="matmul_residual",
    )(a, w, res)


def _final_norm_kernel(x_ref, g_ref, o_ref):
    o_ref[...] = _rms(x_ref[...], g_ref[...])


def final_norm(x, g, tm=512):
    t, d = x.shape
    tm = min(tm, t)
    return pl.pallas_call(
        _final_norm_kernel,
        out_shape=jax.ShapeDtypeStruct((t, d), F32),
        grid=(t // tm,),
        in_specs=[pl.BlockSpec((tm, d), lambda i: (i, 0)), pl.BlockSpec((1, d), lambda i: (0, 0))],
        out_specs=pl.BlockSpec((tm, d), lambda i: (i, 0)),
        compiler_params=_params("parallel"),
        name="final_norm",
    )(x, g)


def _sb_prompt_kernel(bias_ref, q_ref, k_ref, v_ref, o_ref):
    hp = pl.program_id(1)
    qi = pl.program_id(2)
    row = lax.broadcasted_iota(jnp.int32, (Q_TILE, Q_TILE), 0)
    col = lax.broadcasted_iota(jnp.int32, (Q_TILE, Q_TILE), 1)
    suffix = jnp.where(row >= col, 1.0, 0.0).astype(BF16)
    first = lax.broadcasted_iota(jnp.int32, (Q_TILE, 2 * HEAD_DIM), 1) < HEAD_DIM
    q = q_ref[...] * (1.0 / math.sqrt(HEAD_DIM))
    qs = (jnp.where(first, q, 0.0).astype(BF16), jnp.where(first, 0.0, q).astype(BF16))
    biases = (bias_ref[hp * 2], bias_ref[hp * 2 + 1])

    def body(i, carry):
        kb = qi - i
        rows = pl.ds(pl.multiple_of(kb * Q_TILE, Q_TILE), Q_TILE)
        k = k_ref[rows, :].astype(BF16)
        v = v_ref[rows, :].astype(BF16)
        mask = (col + kb * Q_TILE) < (row + qi * Q_TILE)
        new = []
        for hh in range(2):
            acc, cs = carry[2 * hh], carry[2 * hh + 1]
            z = lax.dot_general(qs[hh], k, NT_DIMS, preferred_element_type=F32) + biases[hh]
            ln = jnp.where(mask, -_softplus(z), 0.0)
            hi, lo = _split_bf16(ln)
            incl = (jnp.dot(hi, suffix, preferred_element_type=F32)
                    + jnp.dot(lo, suffix, preferred_element_type=F32) + cs)
            w = jnp.where(mask, jnp.exp(z + incl), 0.0)
            new.append(acc + jnp.dot(w.astype(BF16), v, preferred_element_type=F32))
            new.append(cs + jnp.sum(ln, axis=-1, keepdims=True))
        return tuple(new)

    zero = jnp.zeros((Q_TILE, 2 * HEAD_DIM), F32)
    zcs = jnp.zeros((Q_TILE, 1), F32)
    acc0, _, acc1, _ = lax.fori_loop(0, qi + 1, body, (zero, zcs, zero, zcs))
    o_ref[...] = jnp.where(first, acc0, acc1)


def sb_prompt(qkv, bias, n_batch, seq):
    d = qkv.shape[1] // 3
    n_pair = d // (2 * HEAD_DIM)
    n_q = seq // Q_TILE
    return pl.pallas_call(
        _sb_prompt_kernel,
        out_shape=jax.ShapeDtypeStruct((n_batch * seq, d), F32),
        grid=(n_batch, n_pair, n_q),
        in_specs=[pl.BlockSpec(memory_space=pltpu.SMEM),
                  pl.BlockSpec((Q_TILE, 2 * HEAD_DIM), lambda b, p, i: (b * n_q + i, p)),
                  pl.BlockSpec((seq, 2 * HEAD_DIM), lambda b, p, i: (b, n_pair + p)),
                  pl.BlockSpec((seq, 2 * HEAD_DIM), lambda b, p, i: (b, 2 * n_pair + p))],
        out_specs=pl.BlockSpec((Q_TILE, 2 * HEAD_DIM), lambda b, p, i: (b * n_q + i, p)),
        compiler_params=_params("parallel", "parallel", "arbitrary"),
        name="sb_prompt",
    )(bias, qkv, qkv, qkv)


def _sb_sample_kernel(pt_ref, q_ref, kn_ref, vn_ref, bias_ref, kc_ref, vc_ref, o_ref,
                      qrep_ref, kpad_ref, vpad_ref, acc_ref, cs_ref):
    del pt_ref
    s = pl.program_id(1)
    n_q = q_ref.shape[0]
    d = q_ref.shape[1]
    row = lax.broadcasted_iota(jnp.int32, (PAGE, LANES), 0)
    lane = lax.broadcasted_iota(jnp.int32, (PAGE, LANES), 1)
    suffix_t = jnp.where(lane >= row, 1.0, 0.0).astype(BF16)

    def process(kblk, vblk, mask):
        zt = lax.dot_general(kblk, qrep_ref[...], NT_DIMS, preferred_element_type=F32) + bias_ref[...]
        ln = -_softplus(zt)
        if mask is not None:
            ln = jnp.where(mask, ln, 0.0)
        hi, lo = _split_bf16(ln)
        incl = (jnp.dot(suffix_t, hi, preferred_element_type=F32)
                + jnp.dot(suffix_t, lo, preferred_element_type=F32) + cs_ref[...])
        w = jnp.exp(zt + incl)
        if mask is not None:
            w = jnp.where(mask, w, 0.0)
        acc_ref[...] += lax.dot_general(w.astype(BF16), vblk, TN_DIMS, preferred_element_type=F32)
        cs_ref[...] += jnp.sum(ln, axis=0, keepdims=True)

    @pl.when(s == 0)
    def _():
        q = q_ref[...] * (1.0 / math.sqrt(HEAD_DIM))
        qrep = jnp.concatenate([q] * N_HEADS, axis=0)
        r_head = lax.broadcasted_iota(jnp.int32, (N_HEADS * n_q, d), 0) >> (n_q.bit_length() - 1)
        c_head = lax.broadcasted_iota(jnp.int32, (N_HEADS * n_q, d), 1) >> (HEAD_DIM.bit_length() - 1)
        qrep_ref[...] = jnp.where(r_head == c_head, qrep, 0.0).astype(BF16)
        acc_ref[...] = jnp.zeros_like(acc_ref)
        cs_ref[...] = jnp.zeros_like(cs_ref)
        kpad_ref[...] = jnp.zeros_like(kpad_ref)
        vpad_ref[...] = jnp.zeros_like(vpad_ref)
        kpad_ref[0:n_q, :] = kn_ref[...]
        vpad_ref[0:n_q, :] = vn_ref[...]
        mask = row < (lane & (n_q - 1))
        process(kpad_ref[...].astype(BF16), vpad_ref[...].astype(BF16), mask)

    @pl.when(s > 0)
    def _():
        process(kc_ref[...].astype(BF16), vc_ref[...].astype(BF16), None)

    @pl.when(s == pl.num_programs(1) - 1)
    def _():
        half = lax.broadcasted_iota(jnp.int32, (n_q, LANES), 1) < HEAD_DIM
        outs = []
        for j in range(d // LANES):
            a = acc_ref[2 * j * n_q:(2 * j + 1) * n_q, j * LANES:(j + 1) * LANES]
            b = acc_ref[(2 * j + 1) * n_q:(2 * j + 2) * n_q, j * LANES:(j + 1) * LANES]
            outs.append(jnp.where(half, a, b))
        o_ref[...] = jnp.concatenate(outs, axis=-1)


def sb_sample(qkv, row0, n_batch, n_q, page_ids, bias_lanes, kcache, vcache):
    d = qkv.shape[1] // 3
    n_pages = page_ids.shape[0] // n_batch
    assert N_HEADS * n_q == LANES and n_q == SUBLANES and row0 % n_q == 0
    rb = row0 // n_q

    def page_map(b, s, pt):
        return (pt[b * n_pages + jnp.minimum(n_pages - s, n_pages - 1)], 0, 0)

    grid_spec = pltpu.PrefetchScalarGridSpec(
        num_scalar_prefetch=1,
        grid=(n_batch, n_pages + 1),
        in_specs=[pl.BlockSpec((n_q, d), lambda b, s, pt: (rb + b, 0)),
                  pl.BlockSpec((n_q, d), lambda b, s, pt: (rb + b, 1)),
                  pl.BlockSpec((n_q, d), lambda b, s, pt: (rb + b, 2)),
                  pl.BlockSpec((1, LANES), lambda b, s, pt: (0, 0)),
                  pl.BlockSpec((None, PAGE, d), page_map),
                  pl.BlockSpec((None, PAGE, d), page_map)],
        out_specs=pl.BlockSpec((n_q, d), lambda b, s, pt: (b, 0)),
        scratch_shapes=[pltpu.VMEM((LANES, d), BF16),
                        pltpu.VMEM((PAGE, d), F32),
                        pltpu.VMEM((PAGE, d), F32),
                        pltpu.VMEM((LANES, d), F32),
                        pltpu.VMEM((1, LANES), F32)],
    )
    return pl.pallas_call(
        _sb_sample_kernel,
        out_shape=jax.ShapeDtypeStruct((n_batch * n_q, d), F32),
        grid_spec=grid_spec,
        compiler_params=_params("parallel", "arbitrary"),
        name="sb_sample",
    )(page_ids, qkv, qkv, qkv, bias_lanes, kcache, vcache)


def _shifted(u, k, buf_ref, row):
    nb = buf_ref.shape[0]
    s = pltpu.roll(u, k, axis=0)
    for r in range(k):
        s = jnp.where(row == r, buf_ref[nb - k + r:nb - k + r + 1, :], s)
    return s


def _short_conv_kernel(bg_ref, cg_ref, xv_ref, w_ref, buf_ref, g_ref, nb_ref):
    u = cg_ref[...] * xv_ref[...]
    t = u.shape[0]
    row = lax.broadcasted_iota(jnp.int32, u.shape, 0)
    y = _shifted(u, 2, buf_ref, row) * w_ref[0:1, :]
    y = y + _shifted(u, 1, buf_ref, row) * w_ref[1:2, :]
    y = y + u * w_ref[2:3, :]
    g_ref[...] = bg_ref[...] * y
    nb_ref[...] = u[t - 2:t, :]


def short_conv(proj, row0, n_batch, seq, conv_w, buf, tc=512):
    c = proj.shape[1] // 3
    nc = c // tc
    rb = row0 // seq
    return pl.pallas_call(
        _short_conv_kernel,
        out_shape=(jax.ShapeDtypeStruct((n_batch * seq, c), F32),
                   jax.ShapeDtypeStruct((n_batch, 2, c), F32)),
        grid=(n_batch, nc),
        in_specs=[pl.BlockSpec((seq, tc), lambda b, j: (rb + b, j)),
                  pl.BlockSpec((seq, tc), lambda b, j: (rb + b, nc + j)),
                  pl.BlockSpec((seq, tc), lambda b, j: (rb + b, 2 * nc + j)),
                  pl.BlockSpec((3, tc), lambda b, j: (0, j)),
                  pl.BlockSpec((None, 2, tc), lambda b, j: (b, 0, j))],
        out_specs=(pl.BlockSpec((seq, tc), lambda b, j: (b, j)),
                   pl.BlockSpec((None, 2, tc), lambda b, j: (b, 0, j))),
        compiler_params=_params("parallel", "parallel"),
        name="short_conv",
    )(proj, proj, proj, conv_w, buf)


def _scan_tile(a, d, row):
    for k in (1, 2, 4):
        a_s = pltpu.roll(a, k, axis=0)
        d_s = pltpu.roll(d, k, axis=0)
        valid = row >= k
        d = jnp.where(valid, a * d_s + d, d)
        a = jnp.where(valid, a * a_s, a)
    return a, d


def _rglru_kernel(gate_ref, u_ref, cw_ref, cb_ref, wra_ref, bra_ref, wri_ref, bri_ref, lam_ref,
                  buf_ref, h0_ref, o_ref, nb_ref, hl_ref, a_scr, d_scr):
    up = u_ref[...]
    t, c = up.shape
    row = lax.broadcasted_iota(jnp.int32, up.shape, 0)
    y = _shifted(up, 3, buf_ref, row) * cw_ref[0:1, :]
    y = y + _shifted(up, 2, buf_ref, row) * cw_ref[1:2, :]
    y = y + _shifted(up, 1, buf_ref, row) * cw_ref[2:3, :]
    y = y + up * cw_ref[3:4, :]
    u = y + cb_ref[...]
    rs, igs = [], []
    for n in range(c // RG_BLOCK):
        ub = u[:, n * RG_BLOCK:(n + 1) * RG_BLOCK].astype(BF16)
        rs.append(jnp.dot(ub, wra_ref[n], preferred_element_type=F32))
        igs.append(jnp.dot(ub, wri_ref[n], preferred_element_type=F32))
    r = _sigmoid(jnp.concatenate(rs, axis=-1) + bra_ref[...])
    ig = _sigmoid(jnp.concatenate(igs, axis=-1) + bri_ref[...])
    log_a = -RG_C * r * _softplus(-lam_ref[...])
    a = jnp.exp(log_a)
    drive = jnp.sqrt(-jnp.tanh(log_a) * (a * a + 1.0)) * (ig * u)
    a_scr[...] = a
    d_scr[...] = drive
    row8 = lax.broadcasted_iota(jnp.int32, (SUBLANES, c), 0)

    def body(i, h):
        rows = pl.ds(pl.multiple_of(i * SUBLANES, SUBLANES), SUBLANES)
        a_c, d_c = _scan_tile(a_scr[rows, :], d_scr[rows, :], row8)
        hs = a_c * h + d_c
        d_scr[rows, :] = hs
        return hs[SUBLANES - 1:SUBLANES, :]

    h_last = lax.fori_loop(0, t // SUBLANES, body, h0_ref[...])
    o_ref[...] = _gelu(gate_ref[...]) * d_scr[...]
    nb_ref[...] = up[t - 3:t, :]
    hl_ref[...] = h_last


def rglru(proj, row0, n_batch, seq, tc, conv_w, conv_b, w_ra, b_ra, w_ri, b_ri, lam, buf, h0):
    c = proj.shape[1] // 2
    nc = c // tc
    nblk = tc // RG_BLOCK
    rb = row0 // seq
    vec = pl.BlockSpec((1, tc), lambda b, j: (0, j))
    return pl.pallas_call(
        _rglru_kernel,
        out_shape=(jax.ShapeDtypeStruct((n_batch * seq, c), F32),
                   jax.ShapeDtypeStruct((n_batch, 3, c), F32),
                   jax.ShapeDtypeStruct((n_batch, 1, c), F32)),
        grid=(n_batch, nc),
        in_specs=[pl.BlockSpec((seq, tc), lambda b, j: (rb + b, j)),
                  pl.BlockSpec((seq, tc), lambda b, j: (rb + b, nc + j)),
                  pl.BlockSpec((4, tc), lambda b, j: (0, j)),
                  vec,
                  pl.BlockSpec((nblk, RG_BLOCK, RG_BLOCK), lambda b, j: (j, 0, 0)),
                  vec,
                  pl.BlockSpec((nblk, RG_BLOCK, RG_BLOCK), lambda b, j: (j, 0, 0)),
                  vec,
                  vec,
                  pl.BlockSpec((None, 3, tc), lambda b, j: (b, 0, j)),
                  pl.BlockSpec((None, 1, tc), lambda b, j: (b, 0, j))],
        out_specs=(pl.BlockSpec((seq, tc), lambda b, j: (b, j)),
                   pl.BlockSpec((None, 3, tc), lambda b, j: (b, 0, j)),
                   pl.BlockSpec((None, 1, tc), lambda b, j: (b, 0, j))),
        scratch_shapes=[pltpu.VMEM((seq, tc), F32), pltpu.VMEM((seq, tc), F32)],
        compiler_params=_params("parallel", "parallel"),
        name="rglru",
    )(proj, proj, conv_w, conv_b, w_ra, b_ra, w_ri, b_ri, lam, buf, h0)


def _top16(s):
    n = s.shape[0]
    io = lax.broadcasted_iota(jnp.int32, s.shape, 0)
    io_k = lax.broadcasted_iota(jnp.int32, (TOPK, s.shape[1]), 0)

    def body(i, carry):
        s, rank, vals = carry
        m = jnp.max(s, axis=0, keepdims=True)
        idx = jnp.min(jnp.where(s == m, io, n), axis=0, keepdims=True)
        sel = io == idx
        rank = jnp.where(sel, i, rank)
        s = jnp.where(sel, -jnp.inf, s)
        vals = jnp.where(io_k == i, m, vals)
        return s, rank, vals

    _, rank, vals = lax.fori_loop(
        0, TOPK, body, (s, jnp.full(s.shape, TOPK, jnp.int32), jnp.zeros((TOPK, s.shape[1]), F32)))
    return rank, vals


def _peer_route_kernel(x_ref, g_ref, wq_ref, k1_ref, k2_ref, hn_ref, r2_ref, cnt1_ref, a1_ref, w2_ref, qt_scr):
    h = pl.program_id(1)
    tn = x_ref.shape[0]

    @pl.when(h == 0)
    def _():
        hn = _rms(x_ref[...], g_ref[...]).astype(BF16)
        hn_ref[...] = hn
        qt_scr[...] = lax.dot_general(wq_ref[...], hn, NT_DIMS, preferred_element_type=F32)

    off = pl.multiple_of(h * 2 * N_KEYS, 2 * N_KEYS)
    s1_all = jnp.dot(k1_ref[...], qt_scr[pl.ds(off, N_KEYS), :],
                     precision=lax.Precision.HIGHEST, preferred_element_type=F32)
    s2_all = jnp.dot(k2_ref[...], qt_scr[pl.ds(off + N_KEYS, N_KEYS), :],
                     precision=lax.Precision.HIGHEST, preferred_element_type=F32)
    for lg in range(tn // LANES):
        sl = slice(lg * LANES, (lg + 1) * LANES)
        s1 = s1_all[:, sl]
        s2 = s2_all[:, sl]
        rank1, v1 = _top16(s1)
        rank2, v2 = _top16(s2)
        cand = (v1[:, None, :] + v2[None, :, :]).reshape(TOPK * TOPK, LANES)
        rank_c, _ = _top16(cand)
        sel = rank_c < TOPK
        cnt = jnp.sum(jnp.where(sel, 1.0, 0.0).reshape(TOPK, TOPK, LANES), axis=1)
        z = jnp.sum(jnp.where(sel, jnp.exp(cand - cand[0:1, :]), 0.0), axis=0, keepdims=True)
        cnt1 = jnp.zeros((N_KEYS, LANES), F32)
        for i in range(TOPK):
            cnt1 = jnp.where(rank1 == i, cnt[i:i + 1, :], cnt1)
        r2_ref[:, sl] = rank2.astype(F32)
        cnt1_ref[:, sl] = cnt1
        a1_ref[:, sl] = jnp.exp(s1 - v1[0:1, :])
        w2_ref[:, sl] = jnp.exp(s2 - v2[0:1, :]) / z


def peer_route(x, g, wq_t, keys1, keys2, tn=512):
    t, d = x.shape
    heads = keys1.shape[0]
    tn = min(tn, t)
    side = jax.ShapeDtypeStruct((heads, N_KEYS, t), F32)
    side_spec = pl.BlockSpec((None, N_KEYS, tn), lambda i, h: (h, 0, i))
    return pl.pallas_call(
        _peer_route_kernel,
        out_shape=(jax.ShapeDtypeStruct((t, d), BF16), side, side, side, side),
        grid=(t // tn, heads),
        in_specs=[pl.BlockSpec((tn, d), lambda i, h: (i, 0)),
                  pl.BlockSpec((1, d), lambda i, h: (0, 0)),
                  pl.BlockSpec(wq_t.shape, lambda i, h: (0, 0)),
                  pl.BlockSpec((None, N_KEYS, N_KEYS), lambda i, h: (h, 0, 0)),
                  pl.BlockSpec((None, N_KEYS, N_KEYS), lambda i, h: (h, 0, 0))],
        out_specs=(pl.BlockSpec((tn, d), lambda i, h: (i, 0)), side_spec, side_spec, side_spec, side_spec),
        scratch_shapes=[pltpu.VMEM((wq_t.shape[0], tn), F32)],
        compiler_params=_params("parallel", "arbitrary"),
        name="peer_route",
    )(x, g, wq_t, keys1, keys2)


def _peer_expert_kernel(x_ref, hn_ref, u_ref, vt_ref, r2_ref, cnt1_ref, a1_ref, w2_ref, o_ref,
                        ht_scr, p_scr, acc_scr):
    j = pl.program_id(1)
    heads = r2_ref.shape[0]
    n_grp = u_ref.shape[0] // N_KEYS

    @pl.when(j == 0)
    def _():
        acc_scr[...] = jnp.zeros_like(acc_scr)

    ht_scr[...] = lax.dot_general(u_ref[...], hn_ref[...], NT_DIMS, preferred_element_type=F32)

    def group(e, _):
        rows = pl.ds(pl.multiple_of(e * N_KEYS, N_KEYS), N_KEYS)
        gate = jnp.zeros((N_KEYS, ht_scr.shape[1]), F32)
        for h in range(heads):
            c = cnt1_ref[h, pl.ds(e, 1), :]
            a = a1_ref[h, pl.ds(e, 1), :]
            gate = gate + jnp.where(r2_ref[h] < c, w2_ref[h] * a, 0.0)
        p_scr[rows, :] = (gate * _gelu(ht_scr[rows, :])).astype(BF16)
        return 0

    lax.fori_loop(0, n_grp, group, 0)
    acc_scr[...] += jnp.dot(vt_ref[...], p_scr[...], preferred_element_type=F32)

    @pl.when(j == pl.num_programs(1) - 1)
    def _():
        o_ref[...] = x_ref[...] + acc_scr[...].T


def peer_experts(x, hn, u, vt, r2, cnt1, a1, w2, tn=512, eb=1024):
    t, d = x.shape
    ne = u.shape[0]
    heads = r2.shape[0]
    tn = min(tn, t)
    n_grp = eb // N_KEYS
    full = pl.BlockSpec((heads, N_KEYS, tn), lambda i, j: (0, 0, i))
    part = pl.BlockSpec((heads, n_grp, tn), lambda i, j: (0, j, i))
    return pl.pallas_call(
        _peer_expert_kernel,
        out_shape=jax.ShapeDtypeStruct((t, d), F32),
        grid=(t // tn, ne // eb),
        in_specs=[pl.BlockSpec((tn, d), lambda i, j: (i, 0)),
                  pl.BlockSpec((tn, d), lambda i, j: (i, 0)),
                  pl.BlockSpec((eb, d), lambda i, j: (j, 0)),
                  pl.BlockSpec((d, eb), lambda i, j: (0, j)),
                  full, part, part, full],
        out_specs=pl.BlockSpec((tn, d), lambda i, j: (i, 0)),
        scratch_shapes=[pltpu.VMEM((eb, tn), F32), pltpu.VMEM((eb, tn), BF16), pltpu.VMEM((d, tn), F32)],
        compiler_params=_params("parallel", "arbitrary"),
        name="peer_experts",
    )(x, hn, u, vt, r2, cnt1, a1, w2)


def peer_ffn_residual(x, g, w_query, keys1, keys2, u_table, v_table):
    wq_t = w_query.T.astype(BF16)
    hn, r2, cnt1, a1, w2 = peer_route(x, g, wq_t, keys1, keys2)
    return peer_experts(x, hn, u_table.astype(BF16), v_table.T.astype(BF16), r2, cnt1, a1, w2)


def kernel(x_prompt, x_sample, cache_k_sb, cache_v_sb, page_table, state_conv_sc, state_conv_rg, state_h_rg, norm_mix, norm_ffn, norm_final, w_qkv_sb, w_o_sb, b_sb, w_in_sc, conv_w_sc, w_out_sc, w_in_rg, conv_w_rg, conv_b_rg, w_ra_rg, b_ra_rg, w_ri_rg, b_ri_rg, lam_rg, w_out_rg, w_query_peer, keys1_peer, keys2_peer, u_peer, v_peer):
    bp, seq, d = x_prompt.shape
    bs, dseq, _ = x_sample.shape
    n_p = bp * seq
    depth = norm_mix.shape[0]
    n_pool = cache_k_sb.shape[1]
    n_pages = page_table.shape[1]
    x = jnp.concatenate([x_prompt.reshape(n_p, d), x_sample.reshape(bs * dseq, d)], axis=0)
    kcache = cache_k_sb.reshape(-1, PAGE, d)
    vcache = cache_v_sb.reshape(-1, PAGE, d)
    k_p, v_p, k_s, v_s = [], [], [], []
    csc_p, csc_s, crg_p, crg_s, h_p, h_s = [], [], [], [], [], []
    for layer in range(depth):
        kind, j = layer % 3, layer // 3
        g_mix = norm_mix[layer][None, :]
        if kind == 0:
            qkv = norm_matmul(x, g_mix, w_qkv_sb[j].astype(BF16))
            o_p = sb_prompt(qkv, b_sb[j], bp, seq)
            page_ids = (page_table + j * n_pool).reshape(-1)
            bias_lanes = jnp.repeat(b_sb[j], dseq)[None, :]
            o_s = sb_sample(qkv, n_p, bs, dseq, page_ids, bias_lanes, kcache, vcache)
            mix = jnp.concatenate([o_p, o_s], axis=0)
            w_out = w_o_sb[j]
            k_p.append(qkv[:n_p, d:2 * d].reshape(bp, seq, N_HEADS, HEAD_DIM))
            v_p.append(qkv[:n_p, 2 * d:].reshape(bp, seq, N_HEADS, HEAD_DIM))
            k_s.append(qkv[n_p:, d:2 * d].reshape(bs, dseq, N_HEADS, HEAD_DIM))
            v_s.append(qkv[n_p:, 2 * d:].reshape(bs, dseq, N_HEADS, HEAD_DIM))
        elif kind == 1:
            proj = norm_matmul(x, g_mix, w_in_sc[j].astype(BF16))
            c = conv_w_sc.shape[2]
            g_p, nb_p = short_conv(proj, 0, bp, seq, conv_w_sc[j], jnp.zeros((bp, 2, c), F32))
            g_s, nb_s = short_conv(proj, n_p, bs, dseq, conv_w_sc[j], state_conv_sc[j])
            mix = jnp.concatenate([g_p, g_s], axis=0)
            w_out = w_out_sc[j]
            csc_p.append(nb_p)
            csc_s.append(nb_s)
        else:
            proj = norm_matmul(x, g_mix, w_in_rg[j].astype(BF16))
            c = conv_w_rg.shape[2]
            wts = (conv_w_rg[j], conv_b_rg[j][None, :], w_ra_rg[j].astype(BF16), b_ra_rg[j][None, :],
                   w_ri_rg[j].astype(BF16), b_ri_rg[j][None, :], lam_rg[j][None, :])
            g_p, nb_p, hl_p = rglru(proj, 0, bp, seq, RG_BLOCK, *wts,
                                    jnp.zeros((bp, 3, c), F32), jnp.zeros((bp, 1, c), F32))
            g_s, nb_s, hl_s = rglru(proj, n_p, bs, dseq, c, *wts, state_conv_rg[j], state_h_rg[j][:, None, :])
            mix = jnp.concatenate([g_p, g_s], axis=0)
            w_out = w_out_rg[j]
            crg_p.append(nb_p)
            crg_s.append(nb_s)
            h_p.append(hl_p[:, 0, :])
            h_s.append(hl_s[:, 0, :])
        x = matmul_residual(mix, w_out.astype(BF16), x)
        x = peer_ffn_residual(x, norm_ffn[layer][None, :], w_query_peer[layer], keys1_peer[layer],
                              keys2_peer[layer], u_peer[layer], v_peer[layer])
    y = final_norm(x, norm_final[None, :])
    return (y[:n_p].reshape(bp, seq, d), y[n_p:].reshape(bs, dseq, d),
            jnp.stack(k_p), jnp.stack(v_p), jnp.stack(csc_p), jnp.stack(crg_p), jnp.stack(h_p),
            jnp.stack(k_s), jnp.stack(v_s), jnp.stack(csc_s), jnp.stack(crg_s), jnp.stack(h_s))
```

```python
import functools
import math

import jax
import jax.numpy as jnp
from jax import lax
from jax.experimental import pallas as pl
from jax.experimental.pallas import tpu as pltpu

F32 = jnp.float32
BF16 = jnp.bfloat16
EPS = 1e-6
HEAD_DIM = 64
N_HEADS = 16
PAGE = 128
K_TILE = 128
RG_BLOCK = 128
RG_C = 8.0
TOPK = 16
N_KEYS = 128
EXPERT_CHUNK = 256
LANES = 128
SUBLANES = 8
VMEM_LIMIT = 48 * 1024 * 1024

NT_DIMS = (((1,), (1,)), ((), ()))
TN_DIMS = (((0,), (0,)), ((), ()))


def _params(*sem):
    return pltpu.CompilerParams(dimension_semantics=sem, vmem_limit_bytes=VMEM_LIMIT)


def _softplus(z):
    return jnp.maximum(z, 0.0) + jnp.log(1.0 + jnp.exp(-jnp.abs(z)))


def _sigmoid(z):
    return 1.0 / (1.0 + jnp.exp(-z))


def _gelu(x):
    return 0.5 * x * (1.0 + lax.erf(x * (1.0 / math.sqrt(2.0))))


def _rms(x, g):
    ms = jnp.mean(x * x, axis=-1, keepdims=True)
    return x * lax.rsqrt(ms + EPS) * g


def _split_bf16(x):
    hi = x.astype(BF16)
    lo = (x - hi.astype(F32)).astype(BF16)
    return hi, lo


def _norm_mm_kernel(x_ref, g_ref, w_ref, o_ref, hn_ref):
    @pl.when(pl.program_id(1) == 0)
    def _():
        hn_ref[...] = _rms(x_ref[...], g_ref[...]).astype(BF16)

    o_ref[...] = jnp.dot(hn_ref[...], w_ref[...], preferred_element_type=F32)


def norm_matmul(x, g, w, tm=512, tn=1024):
    t, d = x.shape
    n = w.shape[1]
    tm = min(tm, t)
    tn = min(tn, n)
    return pl.pallas_call(
        _norm_mm_kernel,
        out_shape=jax.ShapeDtypeStruct((t, n), F32),
        grid=(t // tm, n // tn),
        in_specs=[pl.BlockSpec((tm, d), lambda i, j: (i, 0)),
                  pl.BlockSpec((1, d), lambda i, j: (0, 0)),
                  pl.BlockSpec((d, tn), lambda i, j: (0, j))],
        out_specs=pl.BlockSpec((tm, tn), lambda i, j: (i, j)),
        scratch_shapes=[pltpu.VMEM((tm, d), BF16)],
        compiler_params=_params("parallel", "arbitrary"),
        name="norm_matmul",
    )(x, g, w)


def _mm_res_kernel(a_ref, w_ref, r_ref, o_ref):
    o_ref[...] = r_ref[...] + jnp.dot(a_ref[...].astype(BF16), w_ref[...], preferred_element_type=F32)


def matmul_residual(a, w, res, tm=512):
    t, k = a.shape
    n = w.shape[1]
    tm = min(tm, t)
    return pl.pallas_call(
        _mm_res_kernel,
        out_shape=jax.ShapeDtypeStruct((t, n), F32),
        grid=(t // tm,),
        in_specs=[pl.BlockSpec((tm, k), lambda i: (i, 0)),
                  pl.BlockSpec((k, n), lambda i: (0, 0)),
                  pl.BlockSpec((tm, n), lambda i: (i, 0))],
        out_specs=pl.BlockSpec((tm, n), lambda i: (i, 0)),
        compiler_params=_params("parallel"),
        name="matmul_residual",
    )(a, w, res)


def _final_norm_kernel(x_ref, g_ref, o_ref):
    o_ref[...] = _rms(x_ref[...], g_ref[...])


def final_norm(x, g, tm=512):
    t, d = x.shape
    tm = min(tm, t)
    return pl.pallas_call(
        _final_norm_kernel,
        out_shape=jax.ShapeDtypeStruct((t, d), F32),
        grid=(t // tm,),
        in_specs=[pl.BlockSpec((tm, d), lambda i: (i, 0)), pl.BlockSpec((1, d), lambda i: (0, 0))],
        out_specs=pl.BlockSpec((tm, d), lambda i: (i, 0)),
        compiler_params=_params("parallel"),
        name="final_norm",
    )(x, g)


def _sb_prompt_kernel(bias_ref, q_ref, k_ref, v_ref, o_ref, acc_scr, cs_scr):
    hp = pl.program_id(1)
    qi = pl.program_id(2)
    tq = q_ref.shape[0]
    row = lax.broadcasted_iota(jnp.int32, (2 * tq, K_TILE), 0)
    col = lax.broadcasted_iota(jnp.int32, (2 * tq, K_TILE), 1)
    head1 = row >= tq
    q_pos = jnp.where(head1, row - tq, row) + qi * tq
    tri_r = lax.broadcasted_iota(jnp.int32, (2 * K_TILE, K_TILE), 0) & (K_TILE - 1)
    tri_c = lax.broadcasted_iota(jnp.int32, (2 * K_TILE, K_TILE), 1)
    suffix2 = jnp.where(tri_r >= tri_c, 1.0, 0.0).astype(BF16)
    first = lax.broadcasted_iota(jnp.int32, (tq, 2 * HEAD_DIM), 1) < HEAD_DIM
    q = q_ref[...] * (1.0 / math.sqrt(HEAD_DIM))
    q2 = jnp.concatenate([jnp.where(first, q, 0.0), jnp.where(first, 0.0, q)], axis=0).astype(BF16)
    bias = jnp.where(head1, bias_ref[hp * 2 + 1], bias_ref[hp * 2])
    acc_scr[...] = jnp.zeros_like(acc_scr)
    cs_scr[...] = jnp.zeros_like(cs_scr)
    n_kb = (qi + 1) * (tq // K_TILE)

    def body(i, _):
        kb = n_kb - 1 - i
        rows = pl.ds(pl.multiple_of(kb * K_TILE, K_TILE), K_TILE)
        k = k_ref[rows, :].astype(BF16)
        v = v_ref[rows, :].astype(BF16)
        z = lax.dot_general(q2, k, NT_DIMS, preferred_element_type=F32) + bias
        mask = (col + kb * K_TILE) < q_pos
        ln = jnp.where(mask, -_softplus(z), 0.0)
        hi, lo = _split_bf16(ln)
        incl = jnp.dot(jnp.concatenate([hi, lo], axis=1), suffix2, preferred_element_type=F32) + cs_scr[...]
        w = jnp.where(mask, jnp.exp(z + incl), 0.0)
        acc_scr[...] += jnp.dot(w.astype(BF16), v, preferred_element_type=F32)
        cs_scr[...] += jnp.sum(ln, axis=-1, keepdims=True)
        return 0

    lax.fori_loop(0, n_kb, body, 0)
    o_ref[...] = jnp.where(first, acc_scr[0:tq, :], acc_scr[tq:2 * tq, :])


def sb_prompt(qkv, bias, n_batch, seq, tq=256):
    d = qkv.shape[1] // 3
    n_pair = d // (2 * HEAD_DIM)
    tq = min(tq, seq)
    n_q = seq // tq
    return pl.pallas_call(
        _sb_prompt_kernel,
        out_shape=jax.ShapeDtypeStruct((n_batch * seq, d), F32),
        grid=(n_batch, n_pair, n_q),
        in_specs=[pl.BlockSpec(memory_space=pltpu.SMEM),
                  pl.BlockSpec((tq, 2 * HEAD_DIM), lambda b, p, i: (b * n_q + i, p)),
                  pl.BlockSpec((seq, 2 * HEAD_DIM), lambda b, p, i: (b, n_pair + p)),
                  pl.BlockSpec((seq, 2 * HEAD_DIM), lambda b, p, i: (b, 2 * n_pair + p))],
        out_specs=pl.BlockSpec((tq, 2 * HEAD_DIM), lambda b, p, i: (b * n_q + i, p)),
        scratch_shapes=[pltpu.VMEM((2 * tq, 2 * HEAD_DIM), F32), pltpu.VMEM((2 * tq, 1), F32)],
        compiler_params=_params("parallel", "parallel", "arbitrary"),
        name="sb_prompt",
    )(bias, qkv, qkv, qkv)


def _sb_sample_kernel(pt_ref, q_ref, kn_ref, vn_ref, bias_ref, *rest):
    del pt_ref
    pps = (len(rest) - 6) // 2
    kc_refs, vc_refs = rest[:pps], rest[pps:2 * pps]
    o_ref, qrep_ref, kpad_ref, vpad_ref, acc_ref, cs_ref = rest[2 * pps:]
    s = pl.program_id(1)
    n_q = q_ref.shape[0]
    d = q_ref.shape[1]
    row = lax.broadcasted_iota(jnp.int32, (PAGE, LANES), 0)
    lane = lax.broadcasted_iota(jnp.int32, (PAGE, LANES), 1)
    tri_c = lax.broadcasted_iota(jnp.int32, (PAGE, 2 * PAGE), 1) & (PAGE - 1)
    tri_r = lax.broadcasted_iota(jnp.int32, (PAGE, 2 * PAGE), 0)
    suffix_t2 = jnp.where(tri_c >= tri_r, 1.0, 0.0).astype(BF16)

    def process(kblks, vblks, mask):
        zts, lns, sums = [], [], []
        for kblk in kblks:
            zt = lax.dot_general(kblk, qrep_ref[...], NT_DIMS, preferred_element_type=F32) + bias_ref[...]
            ln = -_softplus(zt)
            if mask is not None:
                ln = jnp.where(mask, ln, 0.0)
            zts.append(zt)
            lns.append(ln)
            sums.append(jnp.sum(ln, axis=0, keepdims=True))
        carry = cs_ref[...]
        total = None
        for zt, ln, colsum, vblk in zip(zts, lns, sums, vblks):
            hi, lo = _split_bf16(ln)
            incl = jnp.dot(suffix_t2, jnp.concatenate([hi, lo], axis=0), preferred_element_type=F32) + carry
            w = jnp.exp(zt + incl)
            if mask is not None:
                w = jnp.where(mask, w, 0.0)
            part = lax.dot_general(w.astype(BF16), vblk, TN_DIMS, preferred_element_type=F32)
            total = part if total is None else total + part
            carry = carry + colsum
        acc_ref[...] += total
        cs_ref[...] = carry

    @pl.when(s == 0)
    def _():
        q = q_ref[...] * (1.0 / math.sqrt(HEAD_DIM))
        qrep = jnp.concatenate([q] * N_HEADS, axis=0)
        r_head = lax.broadcasted_iota(jnp.int32, (N_HEADS * n_q, d), 0) >> (n_q.bit_length() - 1)
        c_head = lax.broadcasted_iota(jnp.int32, (N_HEADS * n_q, d), 1) >> (HEAD_DIM.bit_length() - 1)
        qrep_ref[...] = jnp.where(r_head == c_head, qrep, 0.0).astype(BF16)
        acc_ref[...] = jnp.zeros_like(acc_ref)
        cs_ref[...] = jnp.zeros_like(cs_ref)
        kpad_ref[...] = jnp.zeros_like(kpad_ref)
        vpad_ref[...] = jnp.zeros_like(vpad_ref)
        kpad_ref[0:n_q, :] = kn_ref[...]
        vpad_ref[0:n_q, :] = vn_ref[...]
        mask = row < (lane & (n_q - 1))
        process([kpad_ref[...].astype(BF16)], [vpad_ref[...].astype(BF16)], mask)

    @pl.when(s > 0)
    def _():
        process([r[...].astype(BF16) for r in kc_refs], [r[...].astype(BF16) for r in vc_refs], None)

    @pl.when(s == pl.num_programs(1) - 1)
    def _():
        half = lax.broadcasted_iota(jnp.int32, (n_q, LANES), 1) < HEAD_DIM
        outs = []
        for j in range(d // LANES):
            a = acc_ref[2 * j * n_q:(2 * j + 1) * n_q, j * LANES:(j + 1) * LANES]
            b = acc_ref[(2 * j + 1) * n_q:(2 * j + 2) * n_q, j * LANES:(j + 1) * LANES]
            outs.append(jnp.where(half, a, b))
        o_ref[...] = jnp.concatenate(outs, axis=-1)


def sb_sample(qkv, row0, n_batch, n_q, page_ids, bias_lanes, kcache, vcache, pps=4):
    d = qkv.shape[1] // 3
    n_pages = page_ids.shape[0] // n_batch
    pps = math.gcd(pps, n_pages)
    assert N_HEADS * n_q == LANES and n_q == SUBLANES and row0 % n_q == 0
    rb = row0 // n_q

    def page_spec(r):
        def page_map(b, s, pt):
            return (pt[b * n_pages + n_pages - pps * jnp.maximum(s - 1, 0) - 1 - r], 0, 0)
        return pl.BlockSpec((None, PAGE, d), page_map)

    grid_spec = pltpu.PrefetchScalarGridSpec(
        num_scalar_prefetch=1,
        grid=(n_batch, n_pages // pps + 1),
        in_specs=[pl.BlockSpec((n_q, d), lambda b, s, pt: (rb + b, 0)),
                  pl.BlockSpec((n_q, d), lambda b, s, pt: (rb + b, 1)),
                  pl.BlockSpec((n_q, d), lambda b, s, pt: (rb + b, 2)),
                  pl.BlockSpec((1, LANES), lambda b, s, pt: (0, 0))]
                 + [page_spec(r) for r in range(pps)] + [page_spec(r) for r in range(pps)],
        out_specs=pl.BlockSpec((n_q, d), lambda b, s, pt: (b, 0)),
        scratch_shapes=[pltpu.VMEM((LANES, d), BF16),
                        pltpu.VMEM((PAGE, d), F32),
                        pltpu.VMEM((PAGE, d), F32),
                        pltpu.VMEM((LANES, d), F32),
                        pltpu.VMEM((1, LANES), F32)],
    )
    return pl.pallas_call(
        _sb_sample_kernel,
        out_shape=jax.ShapeDtypeStruct((n_batch * n_q, d), F32),
        grid_spec=grid_spec,
        compiler_params=_params("parallel", "arbitrary"),
        name="sb_sample",
    )(page_ids, qkv, qkv, qkv, bias_lanes, *([kcache] * pps), *([vcache] * pps))


def _shifted(u, k, buf_ref, row):
    nb = buf_ref.shape[0]
    s = pltpu.roll(u, k, axis=0)
    for r in range(k):
        s = jnp.where(row == r, buf_ref[nb - k + r:nb - k + r + 1, :], s)
    return s


def _short_conv_kernel(bg_ref, cg_ref, xv_ref, w_ref, buf_ref, g_ref, nb_ref):
    u = cg_ref[...] * xv_ref[...]
    t = u.shape[0]
    row = lax.broadcasted_iota(jnp.int32, u.shape, 0)
    y = _shifted(u, 2, buf_ref, row) * w_ref[0:1, :]
    y = y + _shifted(u, 1, buf_ref, row) * w_ref[1:2, :]
    y = y + u * w_ref[2:3, :]
    g_ref[...] = bg_ref[...] * y
    nb_ref[...] = u[t - 2:t, :]


def short_conv(proj, row0, n_batch, seq, conv_w, buf, tc=512):
    c = proj.shape[1] // 3
    nc = c // tc
    rb = row0 // seq
    return pl.pallas_call(
        _short_conv_kernel,
        out_shape=(jax.ShapeDtypeStruct((n_batch * seq, c), F32),
                   jax.ShapeDtypeStruct((n_batch, 2, c), F32)),
        grid=(n_batch, nc),
        in_specs=[pl.BlockSpec((seq, tc), lambda b, j: (rb + b, j)),
                  pl.BlockSpec((seq, tc), lambda b, j: (rb + b, nc + j)),
                  pl.BlockSpec((seq, tc), lambda b, j: (rb + b, 2 * nc + j)),
                  pl.BlockSpec((3, tc), lambda b, j: (0, j)),
                  pl.BlockSpec((None, 2, tc), lambda b, j: (b, 0, j))],
        out_specs=(pl.BlockSpec((seq, tc), lambda b, j: (b, j)),
                   pl.BlockSpec((None, 2, tc), lambda b, j: (b, 0, j))),
        compiler_params=_params("parallel", "parallel"),
        name="short_conv",
    )(proj, proj, proj, conv_w, buf)


def _scan_tile(a, d, row):
    for k in (1, 2, 4):
        a_s = pltpu.roll(a, k, axis=0)
        d_s = pltpu.roll(d, k, axis=0)
        valid = row >= k
        d = jnp.where(valid, a * d_s + d, d)
        a = jnp.where(valid, a * a_s, a)
    return a, d


def _rglru_kernel(gate_ref, u_ref, cw_ref, cb_ref, wra_ref, bra_ref, wri_ref, bri_ref, lam_ref,
                  buf_ref, h0_ref, o_ref, nb_ref, hl_ref, a_scr, d_scr):
    up = u_ref[...]
    t, c = up.shape
    row = lax.broadcasted_iota(jnp.int32, up.shape, 0)
    y = _shifted(up, 3, buf_ref, row) * cw_ref[0:1, :]
    y = y + _shifted(up, 2, buf_ref, row) * cw_ref[1:2, :]
    y = y + _shifted(up, 1, buf_ref, row) * cw_ref[2:3, :]
    y = y + up * cw_ref[3:4, :]
    u = y + cb_ref[...]
    rs, igs = [], []
    for n in range(c // RG_BLOCK):
        ub = u[:, n * RG_BLOCK:(n + 1) * RG_BLOCK].astype(BF16)
        rs.append(jnp.dot(ub, wra_ref[n], preferred_element_type=F32))
        igs.append(jnp.dot(ub, wri_ref[n], preferred_element_type=F32))
    r = _sigmoid(jnp.concatenate(rs, axis=-1) + bra_ref[...])
    ig = _sigmoid(jnp.concatenate(igs, axis=-1) + bri_ref[...])
    log_a = -RG_C * r * _softplus(-lam_ref[...])
    a = jnp.exp(log_a)
    drive = jnp.sqrt(-jnp.tanh(log_a) * (a * a + 1.0)) * (ig * u)
    a_scr[...] = a
    d_scr[...] = drive
    row8 = lax.broadcasted_iota(jnp.int32, (SUBLANES, c), 0)

    def body(i, h):
        rows = pl.ds(pl.multiple_of(i * SUBLANES, SUBLANES), SUBLANES)
        a_c, d_c = _scan_tile(a_scr[rows, :], d_scr[rows, :], row8)
        hs = a_c * h + d_c
        d_scr[rows, :] = hs
        return hs[SUBLANES - 1:SUBLANES, :]

    h_last = lax.fori_loop(0, t // SUBLANES, body, h0_ref[...])
    o_ref[...] = _gelu(gate_ref[...]) * d_scr[...]
    nb_ref[...] = up[t - 3:t, :]
    hl_ref[...] = h_last


def rglru(proj, row0, n_batch, seq, tc, conv_w, conv_b, w_ra, b_ra, w_ri, b_ri, lam, buf, h0):
    c = proj.shape[1] // 2
    nc = c // tc
    nblk = tc // RG_BLOCK
    rb = row0 // seq
    vec = pl.BlockSpec((1, tc), lambda b, j: (0, j))
    return pl.pallas_call(
        _rglru_kernel,
        out_shape=(jax.ShapeDtypeStruct((n_batch * seq, c), F32),
                   jax.ShapeDtypeStruct((n_batch, 3, c), F32),
                   jax.ShapeDtypeStruct((n_batch, 1, c), F32)),
        grid=(n_batch, nc),
        in_specs=[pl.BlockSpec((seq, tc), lambda b, j: (rb + b, j)),
                  pl.BlockSpec((seq, tc), lambda b, j: (rb + b, nc + j)),
                  pl.BlockSpec((4, tc), lambda b, j: (0, j)),
                  vec,
                  pl.BlockSpec((nblk, RG_BLOCK, RG_BLOCK), lambda b, j: (j, 0, 0)),
                  vec,
                  pl.BlockSpec((nblk, RG_BLOCK, RG_BLOCK), lambda b, j: (j, 0, 0)),
                  vec,
                  vec,
                  pl.BlockSpec((None, 3, tc), lambda b, j: (b, 0, j)),
                  pl.BlockSpec((None, 1, tc), lambda b, j: (b, 0, j))],
        out_specs=(pl.BlockSpec((seq, tc), lambda b, j: (b, j)),
                   pl.BlockSpec((None, 3, tc), lambda b, j: (b, 0, j)),
                   pl.BlockSpec((None, 1, tc), lambda b, j: (b, 0, j))),
        scratch_shapes=[pltpu.VMEM((seq, tc), F32), pltpu.VMEM((seq, tc), F32)],
        compiler_params=_params("parallel", "parallel"),
        name="rglru",
    )(proj, proj, conv_w, conv_b, w_ra, b_ra, w_ri, b_ri, lam, buf, h0)


def _top16(*scores):
    lanes = scores[0].shape[1]
    io_k = lax.broadcasted_iota(jnp.int32, (TOPK, lanes), 0)

    def body(i, carry):
        new = []
        for s, rank, vals in carry:
            n = s.shape[0]
            io = lax.broadcasted_iota(jnp.int32, s.shape, 0)
            m = jnp.max(s, axis=0, keepdims=True)
            idx = jnp.min(jnp.where(s == m, io, n), axis=0, keepdims=True)
            sel = io == idx
            new.append((jnp.where(sel, -jnp.inf, s), jnp.where(sel, i, rank), jnp.where(io_k == i, m, vals)))
        return tuple(new)

    init = tuple((s, jnp.full(s.shape, TOPK, jnp.int32), jnp.zeros((TOPK, lanes), F32)) for s in scores)
    return [(rank, vals) for _, rank, vals in lax.fori_loop(0, TOPK, body, init)]


def _top16_distinct(*scores):
    lanes = scores[0].shape[1]
    io_k = lax.broadcasted_iota(jnp.int32, (TOPK, lanes), 0)

    def body(i, carry):
        new = []
        for s, vals in carry:
            m = jnp.max(s, axis=0, keepdims=True)
            new.append((jnp.where(s == m, -jnp.inf, s), jnp.where(io_k == i, m, vals)))
        return tuple(new)

    init = tuple((s, jnp.zeros((TOPK, lanes), F32)) for s in scores)
    return list(lax.fori_loop(0, TOPK, body, init))


def _removed_count(s_after):
    return jnp.sum(jnp.where(s_after == -jnp.inf, 1.0, 0.0), axis=0, keepdims=True)


def _staircase(v1, v2):
    io8 = lax.broadcasted_iota(jnp.int32, (SUBLANES, v1.shape[1]), 0)
    pad = float(jnp.finfo(F32).min)
    parts = [v1[0:1, :] + v2, v1[1:2, :] + v2[0:SUBLANES, :]]
    for i in range(2, SUBLANES):
        parts.append(jnp.where(io8 < TOPK // (i + 1), v1[i:i + 1, :] + v2[0:SUBLANES, :], pad))
    parts.append(v1[SUBLANES:TOPK, :] + v2[0:1, :])
    return jnp.concatenate(parts, axis=0)


def _staircase_counts(sel):
    rows = [jnp.sum(sel[0:TOPK, :], axis=0, keepdims=True)]
    for i in range(1, SUBLANES):
        lo = TOPK + SUBLANES * (i - 1)
        rows.append(jnp.sum(sel[lo:lo + SUBLANES, :], axis=0, keepdims=True))
    lo = TOPK + SUBLANES * (SUBLANES - 1)
    rows += [sel[lo + r:lo + r + 1, :] for r in range(SUBLANES)]
    return rows


def _peer_route_kernel(x_ref, g_ref, wq_ref, k1_ref, k2_ref, hn_ref, r2_ref, cnt1_ref, a1_ref, w2_ref, qt_scr):
    h = pl.program_id(1)
    tn = x_ref.shape[0]

    @pl.when(h == 0)
    def _():
        hn = _rms(x_ref[...], g_ref[...]).astype(BF16)
        hn_ref[...] = hn
        qt_scr[...] = lax.dot_general(wq_ref[...], hn, NT_DIMS, preferred_element_type=F32)

    off = pl.multiple_of(h * 2 * N_KEYS, 2 * N_KEYS)
    s1_all = jnp.dot(k1_ref[...], qt_scr[pl.ds(off, N_KEYS), :],
                     precision=lax.Precision.HIGHEST, preferred_element_type=F32)
    s2_all = jnp.dot(k2_ref[...], qt_scr[pl.ds(off + N_KEYS, N_KEYS), :],
                     precision=lax.Precision.HIGHEST, preferred_element_type=F32)
    def route(s1, s2, sl, exact):
        if exact:
            (rank1, v1), (rank2, v2) = _top16(s1, s2)
            r2 = rank2.astype(F32)
        else:
            (left1, v1), (left2, v2) = _top16_distinct(s1, s2)
            r2 = jnp.zeros((N_KEYS, LANES), F32)
            for i in range(TOPK):
                r2 = r2 + jnp.where(v2[i:i + 1, :] > s2, 1.0, 0.0)
            r2 = jnp.where(left2 == -jnp.inf, r2, float(TOPK))
        cand = _staircase(v1, v2)
        if exact:
            ((rank_c, _),) = _top16(cand)
            sel = rank_c < TOPK
        else:
            ((left_c, _),) = _top16_distinct(cand)
            sel = left_c == -jnp.inf
        cnt = _staircase_counts(jnp.where(sel, 1.0, 0.0))
        z = jnp.sum(jnp.where(sel, jnp.exp(cand - cand[0:1, :]), 0.0), axis=0, keepdims=True)
        cnt1 = jnp.zeros((N_KEYS, LANES), F32)
        for i in range(TOPK):
            cnt1 = jnp.where((rank1 == i) if exact else (s1 == v1[i:i + 1, :]), cnt[i], cnt1)
        r2_ref[:, sl] = r2.astype(BF16)
        cnt1_ref[:, sl] = cnt1
        a1_ref[:, sl] = jnp.exp(s1 - v1[0:1, :])
        w2_ref[:, sl] = (jnp.exp(s2 - v2[0:1, :]) / z).astype(BF16)
        if exact:
            return None
        off16 = (jnp.abs(_removed_count(left1) - TOPK) + jnp.abs(_removed_count(left2) - TOPK)
                 + jnp.abs(_removed_count(left_c) - TOPK))
        return jnp.max(off16) > 0.0

    for lg in range(tn // LANES):
        sl = slice(lg * LANES, (lg + 1) * LANES)
        s1 = s1_all[:, sl]
        s2 = s2_all[:, sl]
        tie = route(s1, s2, sl, exact=False)

        @pl.when(tie)
        def _(s1=s1, s2=s2, sl=sl):
            route(s1, s2, sl, exact=True)


def peer_route(x, g, wq_t, keys1, keys2, tn=512):
    t, d = x.shape
    heads = keys1.shape[0]
    tn = min(tn, t)
    side = jax.ShapeDtypeStruct((heads, N_KEYS, t), F32)
    side_bf = jax.ShapeDtypeStruct((heads, N_KEYS, t), BF16)
    side_spec = pl.BlockSpec((None, N_KEYS, tn), lambda i, h: (h, 0, i))
    return pl.pallas_call(
        _peer_route_kernel,
        out_shape=(jax.ShapeDtypeStruct((t, d), BF16), side_bf, side, side, side_bf),
        grid=(t // tn, heads),
        in_specs=[pl.BlockSpec((tn, d), lambda i, h: (i, 0)),
                  pl.BlockSpec((1, d), lambda i, h: (0, 0)),
                  pl.BlockSpec(wq_t.shape, lambda i, h: (0, 0)),
                  pl.BlockSpec((None, N_KEYS, N_KEYS), lambda i, h: (h, 0, 0)),
                  pl.BlockSpec((None, N_KEYS, N_KEYS), lambda i, h: (h, 0, 0))],
        out_specs=(pl.BlockSpec((tn, d), lambda i, h: (i, 0)), side_spec, side_spec, side_spec, side_spec),
        scratch_shapes=[pltpu.VMEM((wq_t.shape[0], tn), F32)],
        compiler_params=_params("parallel", "arbitrary"),
        name="peer_route",
    )(x, g, wq_t, keys1, keys2)


def _peer_expert_kernel(x_ref, hn_ref, u_ref, vt_prev_ref, vt_ref, r2_ref, cnt1_ref, a1_ref, w2_ref, o_ref,
                        acc_scr, pa_scr, pb_scr):
    j = pl.program_id(1)
    n_pair = pl.num_programs(1) - 1
    heads = r2_ref.shape[0]
    eb = u_ref.shape[0] // 2

    def build(p_scr, base):
        hn = hn_ref[...]
        for c0 in range(0, eb, EXPERT_CHUNK):
            ht = lax.dot_general(u_ref[base + c0:base + c0 + EXPERT_CHUNK, :], hn, NT_DIMS,
                                 preferred_element_type=F32)
            act = _gelu(ht).astype(BF16)
            gates = []
            for e in range((base + c0) // N_KEYS, (base + c0 + EXPERT_CHUNK) // N_KEYS):
                gate = None
                for h in range(heads):
                    c = cnt1_ref[h, e:e + 1, :].astype(BF16)
                    a = a1_ref[h, e:e + 1, :].astype(BF16)
                    wa = w2_ref[h] * a
                    term = jnp.where(r2_ref[h] < c, wa, jnp.zeros_like(wa))
                    gate = term if gate is None else gate + term
                gates.append(gate)
            p_scr[c0:c0 + EXPERT_CHUNK, :] = jnp.concatenate(gates, axis=0) * act

    @pl.when(j == 0)
    def _():
        acc_scr[...] = jnp.zeros_like(acc_scr)
        pb_scr[...] = jnp.zeros_like(pb_scr)

    @pl.when(j < n_pair)
    def _():
        acc_scr[...] += jnp.dot(vt_prev_ref[...], pb_scr[...], preferred_element_type=F32)
        build(pa_scr, 0)
        acc_scr[...] += jnp.dot(vt_ref[...], pa_scr[...], preferred_element_type=F32)
        build(pb_scr, eb)

    @pl.when(j == n_pair)
    def _():
        acc = acc_scr[...] + jnp.dot(vt_prev_ref[...], pb_scr[...], preferred_element_type=F32)
        o_ref[...] = x_ref[...] + acc.T


def peer_experts(x, hn, u, vt, r2, cnt1, a1, w2, tn=512, eb=1024):
    t, d = x.shape
    ne = u.shape[0]
    heads = r2.shape[0]
    tn = min(tn, t)
    n_pair = ne // (2 * eb)
    n_grp = 2 * eb // N_KEYS
    last = n_pair - 1
    full = pl.BlockSpec((heads, N_KEYS, tn), lambda i, j: (0, 0, i))
    part = pl.BlockSpec((heads, n_grp, tn), lambda i, j: (0, jnp.minimum(j, last), i))
    return pl.pallas_call(
        _peer_expert_kernel,
        out_shape=jax.ShapeDtypeStruct((t, d), F32),
        grid=(t // tn, n_pair + 1),
        in_specs=[pl.BlockSpec((tn, d), lambda i, j: (i, 0)),
                  pl.BlockSpec((tn, d), lambda i, j: (i, 0)),
                  pl.BlockSpec((2 * eb, d), lambda i, j: (jnp.minimum(j, last), 0)),
                  pl.BlockSpec((d, eb), lambda i, j: (0, jnp.maximum(2 * j - 1, 0))),
                  pl.BlockSpec((d, eb), lambda i, j: (0, jnp.minimum(2 * j, 2 * last))),
                  full, part, part, full],
        out_specs=pl.BlockSpec((tn, d), lambda i, j: (i, 0)),
        scratch_shapes=[pltpu.VMEM((d, tn), F32), pltpu.VMEM((eb, tn), BF16), pltpu.VMEM((eb, tn), BF16)],
        compiler_params=_params("parallel", "arbitrary"),
        name="peer_experts",
    )(x, hn, u, vt, vt, r2, cnt1, a1, w2)


def peer_ffn_residual(x, g, w_query, keys1, keys2, u_table, v_table):
    wq_t = w_query.T.astype(BF16)
    hn, r2, cnt1, a1, w2 = peer_route(x, g, wq_t, keys1, keys2)
    return peer_experts(x, hn, u_table.astype(BF16), v_table.T.astype(BF16), r2, cnt1, a1, w2)


def kernel(x_prompt, x_sample, cache_k_sb, cache_v_sb, page_table, state_conv_sc, state_conv_rg, state_h_rg, norm_mix, norm_ffn, norm_final, w_qkv_sb, w_o_sb, b_sb, w_in_sc, conv_w_sc, w_out_sc, w_in_rg, conv_w_rg, conv_b_rg, w_ra_rg, b_ra_rg, w_ri_rg, b_ri_rg, lam_rg, w_out_rg, w_query_peer, keys1_peer, keys2_peer, u_peer, v_peer):
    bp, seq, d = x_prompt.shape
    bs, dseq, _ = x_sample.shape
    n_p = bp * seq
    depth = norm_mix.shape[0]
    n_pool = cache_k_sb.shape[1]
    n_pages = page_table.shape[1]
    x = jnp.concatenate([x_prompt.reshape(n_p, d), x_sample.reshape(bs * dseq, d)], axis=0)
    kcache = cache_k_sb.reshape(-1, PAGE, d).astype(BF16)
    vcache = cache_v_sb.reshape(-1, PAGE, d).astype(BF16)
    k_p, v_p, k_s, v_s = [], [], [], []
    csc_p, csc_s, crg_p, crg_s, h_p, h_s = [], [], [], [], [], []
    for layer in range(depth):
        kind, j = layer % 3, layer // 3
        g_mix = norm_mix[layer][None, :]
        if kind == 0:
            qkv = norm_matmul(x, g_mix, w_qkv_sb[j].astype(BF16))
            o_p = sb_prompt(qkv, b_sb[j], bp, seq)
            page_ids = (page_table + j * n_pool).reshape(-1)
            bias_lanes = jnp.repeat(b_sb[j], dseq)[None, :]
            o_s = sb_sample(qkv, n_p, bs, dseq, page_ids, bias_lanes, kcache, vcache)
            mix = jnp.concatenate([o_p, o_s], axis=0)
            w_out = w_o_sb[j]
            k_p.append(qkv[:n_p, d:2 * d].reshape(bp, seq, N_HEADS, HEAD_DIM))
            v_p.append(qkv[:n_p, 2 * d:].reshape(bp, seq, N_HEADS, HEAD_DIM))
            k_s.append(qkv[n_p:, d:2 * d].reshape(bs, dseq, N_HEADS, HEAD_DIM))
            v_s.append(qkv[n_p:, 2 * d:].reshape(bs, dseq, N_HEADS, HEAD_DIM))
        elif kind == 1:
            proj = norm_matmul(x, g_mix, w_in_sc[j].astype(BF16))
            c = conv_w_sc.shape[2]
            g_p, nb_p = short_conv(proj, 0, bp, seq, conv_w_sc[j], jnp.zeros((bp, 2, c), F32))
            g_s, nb_s = short_conv(proj, n_p, bs, dseq, conv_w_sc[j], state_conv_sc[j])
            mix = jnp.concatenate([g_p, g_s], axis=0)
            w_out = w_out_sc[j]
            csc_p.append(nb_p)
            csc_s.append(nb_s)
        else:
            proj = norm_matmul(x, g_mix, w_in_rg[j].astype(BF16))
            c = conv_w_rg.shape[2]
            wts = (conv_w_rg[j], conv_b_rg[j][None, :], w_ra_rg[j].astype(BF16), b_ra_rg[j][None, :],
                   w_ri_rg[j].astype(BF16), b_ri_rg[j][None, :], lam_rg[j][None, :])
            g_p, nb_p, hl_p = rglru(proj, 0, bp, seq, RG_BLOCK, *wts,
                                    jnp.zeros((bp, 3, c), F32), jnp.zeros((bp, 1, c), F32))
            g_s, nb_s, hl_s = rglru(proj, n_p, bs, dseq, c, *wts, state_conv_rg[j], state_h_rg[j][:, None, :])
            mix = jnp.concatenate([g_p, g_s], axis=0)
            w_out = w_out_rg[j]
            crg_p.append(nb_p)
            crg_s.append(nb_s)
            h_p.append(hl_p[:, 0, :])
            h_s.append(hl_s[:, 0, :])
        x = matmul_residual(mix, w_out.astype(BF16), x)
        x = peer_ffn_residual(x, norm_ffn[layer][None, :], w_query_peer[layer], keys1_peer[layer],
                              keys2_peer[layer], u_peer[layer], v_peer[layer])
    y = final_norm(x, norm_final[None, :])
    return (y[:n_p].reshape(bp, seq, d), y[n_p:].reshape(bs, dseq, d),
            jnp.stack(k_p), jnp.stack(v_p), jnp.stack(csc_p), jnp.stack(crg_p), jnp.stack(h_p),
            jnp.stack(k_s), jnp.stack(v_s), jnp.stack(csc_s), jnp.stack(crg_s), jnp.stack(h_s))
```

```python
import functools
import math

import jax
import jax.numpy as jnp
from jax import lax
from jax.experimental import pallas as pl
from jax.experimental.pallas import tpu as pltpu

F32 = jnp.float32
BF16 = jnp.bfloat16
EPS = 1e-6
HEAD_DIM = 64
N_HEADS = 16
PAGE = 128
K_TILE = 128
OFF_DIAG_UNROLL = 2
RG_BLOCK = 128
RG_C = 8.0
TOPK = 16
N_KEYS = 128
EXPERT_CHUNK = 256
LANES = 128
SUBLANES = 8
VMEM_LIMIT = 48 * 1024 * 1024

NT_DIMS = (((1,), (1,)), ((), ()))
TN_DIMS = (((0,), (0,)), ((), ()))


def _params(*sem):
    return pltpu.CompilerParams(dimension_semantics=sem, vmem_limit_bytes=VMEM_LIMIT)


def _softplus(z):
    return jnp.maximum(z, 0.0) + jnp.log(1.0 + jnp.exp(-jnp.abs(z)))


def _sigmoid(z):
    return 1.0 / (1.0 + jnp.exp(-z))


def _gelu(x):
    return 0.5 * x * (1.0 + lax.erf(x * (1.0 / math.sqrt(2.0))))


def _rms(x, g):
    ms = jnp.mean(x * x, axis=-1, keepdims=True)
    return x * lax.rsqrt(ms + EPS) * g


def _split_bf16(x):
    hi = x.astype(BF16)
    lo = (x - hi.astype(F32)).astype(BF16)
    return hi, lo


def _norm_mm_kernel(x_ref, g_ref, w_ref, o_ref, hn_ref):
    @pl.when(pl.program_id(1) == 0)
    def _():
        hn_ref[...] = _rms(x_ref[...], g_ref[...]).astype(BF16)

    o_ref[...] = jnp.dot(hn_ref[...], w_ref[...], preferred_element_type=F32)


def norm_matmul(x, g, w, tm=512, tn=1024):
    t, d = x.shape
    n = w.shape[1]
    tm = min(tm, t)
    tn = min(tn, n)
    return pl.pallas_call(
        _norm_mm_kernel,
        out_shape=jax.ShapeDtypeStruct((t, n), F32),
        grid=(t // tm, n // tn),
        in_specs=[pl.BlockSpec((tm, d), lambda i, j: (i, 0)),
                  pl.BlockSpec((1, d), lambda i, j: (0, 0)),
                  pl.BlockSpec((d, tn), lambda i, j: (0, j))],
        out_specs=pl.BlockSpec((tm, tn), lambda i, j: (i, j)),
        scratch_shapes=[pltpu.VMEM((tm, d), BF16)],
        compiler_params=_params("parallel", "arbitrary"),
        name="norm_matmul",
    )(x, g, w)


def _mm_res_kernel(a_ref, w_ref, r_ref, o_ref):
    o_ref[...] = r_ref[...] + jnp.dot(a_ref[...].astype(BF16), w_ref[...], preferred_element_type=F32)


def matmul_residual(a, w, res, tm=512):
    t, k = a.shape
    n = w.shape[1]
    tm = min(tm, t)
    return pl.pallas_call(
        _mm_res_kernel,
        out_shape=jax.ShapeDtypeStruct((t, n), F32),
        grid=(t // tm,),
        in_specs=[pl.BlockSpec((tm, k), lambda i: (i, 0)),
                  pl.BlockSpec((k, n), lambda i: (0, 0)),
                  pl.BlockSpec((tm, n), lambda i: (i, 0))],
        out_specs=pl.BlockSpec((tm, n), lambda i: (i, 0)),
        compiler_params=_params("parallel"),
        name="matmul_residual",
    )(a, w, res)


def _final_norm_kernel(x_ref, g_ref, o_ref):
    o_ref[...] = _rms(x_ref[...], g_ref[...])


def final_norm(x, g, tm=512):
    t, d = x.shape
    tm = min(tm, t)
    return pl.pallas_call(
        _final_norm_kernel,
        out_shape=jax.ShapeDtypeStruct((t, d), F32),
        grid=(t // tm,),
        in_specs=[pl.BlockSpec((tm, d), lambda i: (i, 0)), pl.BlockSpec((1, d), lambda i: (0, 0))],
        out_specs=pl.BlockSpec((tm, d), lambda i: (i, 0)),
        compiler_params=_params("parallel"),
        name="final_norm",
    )(x, g)


def _sb_prompt_kernel(bias_ref, q_ref, k_ref, v_ref, o_ref, acc_scr, cs_scr):
    hp = pl.program_id(1)
    qi = pl.program_id(2)
    tq = q_ref.shape[0]
    row = lax.broadcasted_iota(jnp.int32, (2 * tq, K_TILE), 0)
    col = lax.broadcasted_iota(jnp.int32, (2 * tq, K_TILE), 1)
    head1 = row >= tq
    q_pos = jnp.where(head1, row - tq, row) + qi * tq
    tri_r = lax.broadcasted_iota(jnp.int32, (2 * K_TILE, K_TILE), 0) & (K_TILE - 1)
    tri_c = lax.broadcasted_iota(jnp.int32, (2 * K_TILE, K_TILE), 1)
    suffix2 = jnp.where(tri_r >= tri_c, 1.0, 0.0).astype(BF16)
    first = lax.broadcasted_iota(jnp.int32, (tq, 2 * HEAD_DIM), 1) < HEAD_DIM
    q = q_ref[...] * (1.0 / math.sqrt(HEAD_DIM))
    q2 = jnp.concatenate([jnp.where(first, q, 0.0), jnp.where(first, 0.0, q)], axis=0).astype(BF16)
    bias = jnp.where(head1, bias_ref[hp * 2 + 1], bias_ref[hp * 2])
    n_diag = tq // K_TILE
    n_kb = (qi + 1) * n_diag

    def tile(kb, cs, masked):
        rows = pl.ds(pl.multiple_of(kb * K_TILE, K_TILE), K_TILE)
        k = k_ref[rows, :].astype(BF16)
        v = v_ref[rows, :].astype(BF16)
        z = lax.dot_general(q2, k, NT_DIMS, preferred_element_type=F32) + bias
        ln = -_softplus(z)
        if masked:
            mask = (col + kb * K_TILE) < q_pos
            ln = jnp.where(mask, ln, 0.0)
        hi, lo = _split_bf16(ln)
        part = jnp.dot(jnp.concatenate([hi, lo], axis=1), suffix2, preferred_element_type=F32)
        w = jnp.exp(z + (part + cs))
        if masked:
            w = jnp.where(mask, w, 0.0)
        return jnp.dot(w.astype(BF16), v, preferred_element_type=F32), cs + part[:, 0:1]

    cs = jnp.zeros((2 * tq, 1), F32)
    acc = None
    for t in range(n_diag):
        pv, cs = tile(n_kb - 1 - t, cs, True)
        acc = pv if acc is None else acc + pv
    acc_scr[...] = acc
    cs_scr[...] = cs

    def body(i, _):
        kb = n_kb - n_diag - 1 - OFF_DIAG_UNROLL * i
        cs = cs_scr[...]
        total = None
        for t in range(OFF_DIAG_UNROLL):
            pv, cs = tile(kb - t, cs, False)
            total = pv if total is None else total + pv
        acc_scr[...] += total
        cs_scr[...] = cs
        return 0

    lax.fori_loop(0, (n_kb - n_diag) // OFF_DIAG_UNROLL, body, 0)
    o_ref[...] = jnp.where(first, acc_scr[0:tq, :], acc_scr[tq:2 * tq, :])


def sb_prompt(qkv, bias, n_batch, seq, tq=256):
    d = qkv.shape[1] // 3
    n_pair = d // (2 * HEAD_DIM)
    tq = min(tq, seq)
    assert (tq // K_TILE) % OFF_DIAG_UNROLL == 0
    n_q = seq // tq
    return pl.pallas_call(
        _sb_prompt_kernel,
        out_shape=jax.ShapeDtypeStruct((n_batch * seq, d), F32),
        grid=(n_batch, n_pair, n_q),
        in_specs=[pl.BlockSpec(memory_space=pltpu.SMEM),
                  pl.BlockSpec((tq, 2 * HEAD_DIM), lambda b, p, i: (b * n_q + i, p)),
                  pl.BlockSpec((seq, 2 * HEAD_DIM), lambda b, p, i: (b, n_pair + p)),
                  pl.BlockSpec((seq, 2 * HEAD_DIM), lambda b, p, i: (b, 2 * n_pair + p))],
        out_specs=pl.BlockSpec((tq, 2 * HEAD_DIM), lambda b, p, i: (b * n_q + i, p)),
        scratch_shapes=[pltpu.VMEM((2 * tq, 2 * HEAD_DIM), F32), pltpu.VMEM((2 * tq, 1), F32)],
        compiler_params=_params("parallel", "parallel", "arbitrary"),
        name="sb_prompt",
    )(bias, qkv, qkv, qkv)


def _sb_sample_kernel(pt_ref, q_ref, kn_ref, vn_ref, bias_ref, *rest):
    del pt_ref
    pps = (len(rest) - 6) // 2
    kc_refs, vc_refs = rest[:pps], rest[pps:2 * pps]
    o_ref, qrep_ref, kpad_ref, vpad_ref, acc_ref, cs_ref = rest[2 * pps:]
    s = pl.program_id(1)
    n_q = q_ref.shape[0]
    d = q_ref.shape[1]
    row = lax.broadcasted_iota(jnp.int32, (PAGE, LANES), 0)
    lane = lax.broadcasted_iota(jnp.int32, (PAGE, LANES), 1)
    tri_c = lax.broadcasted_iota(jnp.int32, (PAGE, 2 * PAGE), 1) & (PAGE - 1)
    tri_r = lax.broadcasted_iota(jnp.int32, (PAGE, 2 * PAGE), 0)
    suffix_t2 = jnp.where(tri_c >= tri_r, 1.0, 0.0).astype(BF16)

    def process(kblks, vblks, mask):
        zts, lns, sums = [], [], []
        for kblk in kblks:
            zt = lax.dot_general(kblk, qrep_ref[...], NT_DIMS, preferred_element_type=F32) + bias_ref[...]
            ln = -_softplus(zt)
            if mask is not None:
                ln = jnp.where(mask, ln, 0.0)
            zts.append(zt)
            lns.append(ln)
            sums.append(jnp.sum(ln, axis=0, keepdims=True))
        carry = cs_ref[...]
        total = None
        for zt, ln, colsum, vblk in zip(zts, lns, sums, vblks):
            hi, lo = _split_bf16(ln)
            incl = jnp.dot(suffix_t2, jnp.concatenate([hi, lo], axis=0), preferred_element_type=F32) + carry
            w = jnp.exp(zt + incl)
            if mask is not None:
                w = jnp.where(mask, w, 0.0)
            part = lax.dot_general(w.astype(BF16), vblk, TN_DIMS, preferred_element_type=F32)
            total = part if total is None else total + part
            carry = carry + colsum
        acc_ref[...] += total
        cs_ref[...] = carry

    @pl.when(s == 0)
    def _():
        q = q_ref[...] * (1.0 / math.sqrt(HEAD_DIM))
        qrep = jnp.concatenate([q] * N_HEADS, axis=0)
        r_head = lax.broadcasted_iota(jnp.int32, (N_HEADS * n_q, d), 0) >> (n_q.bit_length() - 1)
        c_head = lax.broadcasted_iota(jnp.int32, (N_HEADS * n_q, d), 1) >> (HEAD_DIM.bit_length() - 1)
        qrep_ref[...] = jnp.where(r_head == c_head, qrep, 0.0).astype(BF16)
        acc_ref[...] = jnp.zeros_like(acc_ref)
        cs_ref[...] = jnp.zeros_like(cs_ref)
        kpad_ref[...] = jnp.zeros_like(kpad_ref)
        vpad_ref[...] = jnp.zeros_like(vpad_ref)
        kpad_ref[0:n_q, :] = kn_ref[...]
        vpad_ref[0:n_q, :] = vn_ref[...]
        mask = row < (lane & (n_q - 1))
        process([kpad_ref[...].astype(BF16)], [vpad_ref[...].astype(BF16)], mask)

    @pl.when(s > 0)
    def _():
        process([r[...].reshape(PAGE, d).astype(BF16) for r in kc_refs],
                [r[...].reshape(PAGE, d).astype(BF16) for r in vc_refs], None)

    @pl.when(s == pl.num_programs(1) - 1)
    def _():
        half = lax.broadcasted_iota(jnp.int32, (n_q, LANES), 1) < HEAD_DIM
        outs = []
        for j in range(d // LANES):
            a = acc_ref[2 * j * n_q:(2 * j + 1) * n_q, j * LANES:(j + 1) * LANES]
            b = acc_ref[(2 * j + 1) * n_q:(2 * j + 2) * n_q, j * LANES:(j + 1) * LANES]
            outs.append(jnp.where(half, a, b))
        o_ref[...] = jnp.concatenate(outs, axis=-1)


def sb_sample(qkv, row0, n_batch, n_q, page_ids, bias_lanes, kcache, vcache, pps=4):
    d = qkv.shape[1] // 3
    n_pages = page_ids.shape[0] // n_batch
    pps = math.gcd(pps, n_pages)
    assert N_HEADS * n_q == LANES and n_q == SUBLANES and row0 % n_q == 0
    rb = row0 // n_q

    def page_spec(r):
        def page_map(b, s, pt):
            return (pt[b * n_pages + n_pages - pps * jnp.maximum(s - 1, 0) - 1 - r], 0, 0, 0)
        return pl.BlockSpec((None, PAGE, N_HEADS, HEAD_DIM), page_map)

    grid_spec = pltpu.PrefetchScalarGridSpec(
        num_scalar_prefetch=1,
        grid=(n_batch, n_pages // pps + 1),
        in_specs=[pl.BlockSpec((n_q, d), lambda b, s, pt: (rb + b, 0)),
                  pl.BlockSpec((n_q, d), lambda b, s, pt: (rb + b, 1)),
                  pl.BlockSpec((n_q, d), lambda b, s, pt: (rb + b, 2)),
                  pl.BlockSpec((1, LANES), lambda b, s, pt: (0, 0))]
                 + [page_spec(r) for r in range(pps)] + [page_spec(r) for r in range(pps)],
        out_specs=pl.BlockSpec((n_q, d), lambda b, s, pt: (b, 0)),
        scratch_shapes=[pltpu.VMEM((LANES, d), BF16),
                        pltpu.VMEM((PAGE, d), F32),
                        pltpu.VMEM((PAGE, d), F32),
                        pltpu.VMEM((LANES, d), F32),
                        pltpu.VMEM((1, LANES), F32)],
    )
    return pl.pallas_call(
        _sb_sample_kernel,
        out_shape=jax.ShapeDtypeStruct((n_batch * n_q, d), F32),
        grid_spec=grid_spec,
        compiler_params=_params("parallel", "arbitrary"),
        name="sb_sample",
    )(page_ids, qkv, qkv, qkv, bias_lanes, *([kcache] * pps), *([vcache] * pps))


def _shifted(u, k, buf_ref, row):
    nb = buf_ref.shape[0]
    s = pltpu.roll(u, k, axis=0)
    for r in range(k):
        s = jnp.where(row == r, buf_ref[nb - k + r:nb - k + r + 1, :], s)
    return s


def _short_conv_kernel(bg_ref, cg_ref, xv_ref, w_ref, buf_ref, g_ref, nb_ref):
    u = cg_ref[...] * xv_ref[...]
    t = u.shape[0]
    row = lax.broadcasted_iota(jnp.int32, u.shape, 0)
    y = _shifted(u, 2, buf_ref, row) * w_ref[0:1, :]
    y = y + _shifted(u, 1, buf_ref, row) * w_ref[1:2, :]
    y = y + u * w_ref[2:3, :]
    g_ref[...] = bg_ref[...] * y
    nb_ref[...] = u[t - 2:t, :]


def short_conv(proj, row0, n_batch, seq, conv_w, buf, tc=512):
    c = proj.shape[1] // 3
    nc = c // tc
    rb = row0 // seq
    return pl.pallas_call(
        _short_conv_kernel,
        out_shape=(jax.ShapeDtypeStruct((n_batch * seq, c), F32),
                   jax.ShapeDtypeStruct((n_batch, 2, c), F32)),
        grid=(n_batch, nc),
        in_specs=[pl.BlockSpec((seq, tc), lambda b, j: (rb + b, j)),
                  pl.BlockSpec((seq, tc), lambda b, j: (rb + b, nc + j)),
                  pl.BlockSpec((seq, tc), lambda b, j: (rb + b, 2 * nc + j)),
                  pl.BlockSpec((3, tc), lambda b, j: (0, j)),
                  pl.BlockSpec((None, 2, tc), lambda b, j: (b, 0, j))],
        out_specs=(pl.BlockSpec((seq, tc), lambda b, j: (b, j)),
                   pl.BlockSpec((None, 2, tc), lambda b, j: (b, 0, j))),
        compiler_params=_params("parallel", "parallel"),
        name="short_conv",
    )(proj, proj, proj, conv_w, buf)


def _scan_tile(a, d, row):
    for k in (1, 2, 4):
        a_s = pltpu.roll(a, k, axis=0)
        d_s = pltpu.roll(d, k, axis=0)
        valid = row >= k
        d = jnp.where(valid, a * d_s + d, d)
        a = jnp.where(valid, a * a_s, a)
    return a, d


def _rglru_kernel(gate_ref, u_ref, cw_ref, cb_ref, wra_ref, bra_ref, wri_ref, bri_ref, lam_ref,
                  buf_ref, h0_ref, o_ref, nb_ref, hl_ref, a_scr, d_scr):
    up = u_ref[...]
    t, c = up.shape
    row = lax.broadcasted_iota(jnp.int32, up.shape, 0)
    y = _shifted(up, 3, buf_ref, row) * cw_ref[0:1, :]
    y = y + _shifted(up, 2, buf_ref, row) * cw_ref[1:2, :]
    y = y + _shifted(up, 1, buf_ref, row) * cw_ref[2:3, :]
    y = y + up * cw_ref[3:4, :]
    u = y + cb_ref[...]
    rs, igs = [], []
    for n in range(c // RG_BLOCK):
        ub = u[:, n * RG_BLOCK:(n + 1) * RG_BLOCK].astype(BF16)
        rs.append(jnp.dot(ub, wra_ref[n], preferred_element_type=F32))
        igs.append(jnp.dot(ub, wri_ref[n], preferred_element_type=F32))
    r = _sigmoid(jnp.concatenate(rs, axis=-1) + bra_ref[...])
    ig = _sigmoid(jnp.concatenate(igs, axis=-1) + bri_ref[...])
    log_a = -RG_C * r * _softplus(-lam_ref[...])
    a = jnp.exp(log_a)
    drive = jnp.sqrt(-jnp.tanh(log_a) * (a * a + 1.0)) * (ig * u)
    a_scr[...] = a
    d_scr[...] = drive
    row8 = lax.broadcasted_iota(jnp.int32, (SUBLANES, c), 0)

    def body(i, h):
        rows = pl.ds(pl.multiple_of(i * SUBLANES, SUBLANES), SUBLANES)
        a_c, d_c = _scan_tile(a_scr[rows, :], d_scr[rows, :], row8)
        hs = a_c * h + d_c
        d_scr[rows, :] = hs
        return hs[SUBLANES - 1:SUBLANES, :]

    h_last = lax.fori_loop(0, t // SUBLANES, body, h0_ref[...])
    o_ref[...] = _gelu(gate_ref[...]) * d_scr[...]
    nb_ref[...] = up[t - 3:t, :]
    hl_ref[...] = h_last


def rglru(proj, row0, n_batch, seq, tc, conv_w, conv_b, w_ra, b_ra, w_ri, b_ri, lam, buf, h0):
    c = proj.shape[1] // 2
    nc = c // tc
    nblk = tc // RG_BLOCK
    rb = row0 // seq
    vec = pl.BlockSpec((1, tc), lambda b, j: (0, j))
    return pl.pallas_call(
        _rglru_kernel,
        out_shape=(jax.ShapeDtypeStruct((n_batch * seq, c), F32),
                   jax.ShapeDtypeStruct((n_batch, 3, c), F32),
                   jax.ShapeDtypeStruct((n_batch, 1, c), F32)),
        grid=(n_batch, nc),
        in_specs=[pl.BlockSpec((seq, tc), lambda b, j: (rb + b, j)),
                  pl.BlockSpec((seq, tc), lambda b, j: (rb + b, nc + j)),
                  pl.BlockSpec((4, tc), lambda b, j: (0, j)),
                  vec,
                  pl.BlockSpec((nblk, RG_BLOCK, RG_BLOCK), lambda b, j: (j, 0, 0)),
                  vec,
                  pl.BlockSpec((nblk, RG_BLOCK, RG_BLOCK), lambda b, j: (j, 0, 0)),
                  vec,
                  vec,
                  pl.BlockSpec((None, 3, tc), lambda b, j: (b, 0, j)),
                  pl.BlockSpec((None, 1, tc), lambda b, j: (b, 0, j))],
        out_specs=(pl.BlockSpec((seq, tc), lambda b, j: (b, j)),
                   pl.BlockSpec((None, 3, tc), lambda b, j: (b, 0, j)),
                   pl.BlockSpec((None, 1, tc), lambda b, j: (b, 0, j))),
        scratch_shapes=[pltpu.VMEM((seq, tc), F32), pltpu.VMEM((seq, tc), F32)],
        compiler_params=_params("parallel", "parallel"),
        name="rglru",
    )(proj, proj, conv_w, conv_b, w_ra, b_ra, w_ri, b_ri, lam, buf, h0)


def _top16(*scores):
    lanes = scores[0].shape[1]
    io_k = lax.broadcasted_iota(jnp.int32, (TOPK, lanes), 0)

    def body(i, carry):
        new = []
        for s, rank, vals in carry:
            n = s.shape[0]
            io = lax.broadcasted_iota(jnp.int32, s.shape, 0)
            m = jnp.max(s, axis=0, keepdims=True)
            idx = jnp.min(jnp.where(s == m, io, n), axis=0, keepdims=True)
            sel = io == idx
            new.append((jnp.where(sel, -jnp.inf, s), jnp.where(sel, i, rank), jnp.where(io_k == i, m, vals)))
        return tuple(new)

    init = tuple((s, jnp.full(s.shape, TOPK, jnp.int32), jnp.zeros((TOPK, lanes), F32)) for s in scores)
    return [(rank, vals) for _, rank, vals in lax.fori_loop(0, TOPK, body, init)]


def _top16_distinct(*scores):
    lanes = scores[0].shape[1]
    io_k = lax.broadcasted_iota(jnp.int32, (TOPK, lanes), 0)

    def body(i, carry):
        new = []
        for s, vals in carry:
            m = jnp.max(s, axis=0, keepdims=True)
            new.append((jnp.where(s == m, -jnp.inf, s), jnp.where(io_k == i, m, vals)))
        return tuple(new)

    init = tuple((s, jnp.zeros((TOPK, lanes), F32)) for s in scores)
    return list(lax.fori_loop(0, TOPK, body, init))


def _removed_count(s_after):
    return jnp.sum(jnp.where(s_after == -jnp.inf, 1.0, 0.0), axis=0, keepdims=True)


def _staircase(v1, v2):
    io8 = lax.broadcasted_iota(jnp.int32, (SUBLANES, v1.shape[1]), 0)
    pad = float(jnp.finfo(F32).min)
    parts = [v1[0:1, :] + v2, v1[1:2, :] + v2[0:SUBLANES, :]]
    for i in range(2, SUBLANES):
        parts.append(jnp.where(io8 < TOPK // (i + 1), v1[i:i + 1, :] + v2[0:SUBLANES, :], pad))
    parts.append(v1[SUBLANES:TOPK, :] + v2[0:1, :])
    return jnp.concatenate(parts, axis=0)


def _staircase_counts(sel):
    rows = [jnp.sum(sel[0:TOPK, :], axis=0, keepdims=True)]
    for i in range(1, SUBLANES):
        lo = TOPK + SUBLANES * (i - 1)
        rows.append(jnp.sum(sel[lo:lo + SUBLANES, :], axis=0, keepdims=True))
    lo = TOPK + SUBLANES * (SUBLANES - 1)
    rows += [sel[lo + r:lo + r + 1, :] for r in range(SUBLANES)]
    return rows


def _peer_route_kernel(x_ref, g_ref, wq_ref, k1_ref, k2_ref, hnt_ref, r2_ref, cnt1_ref, a1_ref, w2_ref, qt_scr):
    h = pl.program_id(1)
    tn = x_ref.shape[0]

    @pl.when(h == 0)
    def _():
        hn_t = _rms(x_ref[...], g_ref[...]).T.astype(BF16)
        hnt_ref[...] = hn_t
        qt_scr[...] = jnp.dot(wq_ref[...], hn_t, preferred_element_type=F32)

    off = pl.multiple_of(h * 2 * N_KEYS, 2 * N_KEYS)
    s1_all = jnp.dot(k1_ref[...], qt_scr[pl.ds(off, N_KEYS), :],
                     precision=lax.Precision.HIGHEST, preferred_element_type=F32)
    s2_all = jnp.dot(k2_ref[...], qt_scr[pl.ds(off + N_KEYS, N_KEYS), :],
                     precision=lax.Precision.HIGHEST, preferred_element_type=F32)

    def route(s1, s2, sl, exact):
        if exact:
            (rank1, v1), (rank2, v2) = _top16(s1, s2)
            r2 = rank2.astype(F32)
        else:
            (left1, v1), (left2, v2) = _top16_distinct(s1, s2)
            r2 = jnp.zeros((N_KEYS, LANES), F32)
            for i in range(TOPK):
                r2 = r2 + jnp.where(v2[i:i + 1, :] > s2, 1.0, 0.0)
            r2 = jnp.where(left2 == -jnp.inf, r2, float(TOPK))
        cand = _staircase(v1, v2)
        if exact:
            ((rank_c, _),) = _top16(cand)
            sel = rank_c < TOPK
        else:
            ((left_c, _),) = _top16_distinct(cand)
            sel = left_c == -jnp.inf
        cnt = _staircase_counts(jnp.where(sel, 1.0, 0.0))
        z = jnp.sum(jnp.where(sel, jnp.exp(cand - cand[0:1, :]), 0.0), axis=0, keepdims=True)
        cnt1 = jnp.zeros((N_KEYS, LANES), F32)
        for i in range(TOPK):
            cnt1 = jnp.where((rank1 == i) if exact else (s1 == v1[i:i + 1, :]), cnt[i], cnt1)
        r2_ref[:, sl] = r2.astype(BF16)
        cnt1_ref[:, sl] = cnt1
        a1_ref[:, sl] = jnp.exp(s1 - v1[0:1, :])
        w2_ref[:, sl] = (jnp.exp(s2 - v2[0:1, :]) / z).astype(BF16)
        if exact:
            return None
        return (jnp.abs(_removed_count(left1) - TOPK) + jnp.abs(_removed_count(left2) - TOPK)
                + jnp.abs(_removed_count(left_c) - TOPK))

    groups = [slice(lg * LANES, (lg + 1) * LANES) for lg in range(tn // LANES)]
    ties = None
    for sl in groups:
        off16 = route(s1_all[:, sl], s2_all[:, sl], sl, exact=False)
        ties = off16 if ties is None else ties + off16

    @pl.when(jnp.max(ties) > 0.0)
    def _():
        for sl in groups:
            route(s1_all[:, sl], s2_all[:, sl], sl, exact=True)


def peer_route(x, g, wq_t, keys1, keys2, tn=512):
    t, d = x.shape
    heads = keys1.shape[0]
    tn = min(tn, t)
    side = jax.ShapeDtypeStruct((heads, N_KEYS, t), F32)
    side_bf = jax.ShapeDtypeStruct((heads, N_KEYS, t), BF16)
    side_spec = pl.BlockSpec((None, N_KEYS, tn), lambda i, h: (h, 0, i))
    return pl.pallas_call(
        _peer_route_kernel,
        out_shape=(jax.ShapeDtypeStruct((d, t), BF16), side_bf, side, side, side_bf),
        grid=(t // tn, heads),
        in_specs=[pl.BlockSpec((tn, d), lambda i, h: (i, 0)),
                  pl.BlockSpec((1, d), lambda i, h: (0, 0)),
                  pl.BlockSpec(wq_t.shape, lambda i, h: (0, 0)),
                  pl.BlockSpec((None, N_KEYS, N_KEYS), lambda i, h: (h, 0, 0)),
                  pl.BlockSpec((None, N_KEYS, N_KEYS), lambda i, h: (h, 0, 0))],
        out_specs=(pl.BlockSpec((d, tn), lambda i, h: (0, i)), side_spec, side_spec, side_spec, side_spec),
        scratch_shapes=[pltpu.VMEM((wq_t.shape[0], tn), F32)],
        compiler_params=_params("parallel", "arbitrary"),
        name="peer_route",
    )(x, g, wq_t, keys1, keys2)


def _peer_expert_kernel(x_ref, hnt_ref, u_ref, vt_prev_ref, vt_ref, r2_ref, cnt1_ref, a1_ref, w2_ref, o_ref,
                        acc_scr, pa_scr, pb_scr):
    j = pl.program_id(1)
    n_pair = pl.num_programs(1) - 1
    heads = r2_ref.shape[0]
    eb = u_ref.shape[0] // 2

    def build(p_scr, base):
        hn_t = hnt_ref[...]
        for c0 in range(0, eb, EXPERT_CHUNK):
            ht = jnp.dot(u_ref[base + c0:base + c0 + EXPERT_CHUNK, :], hn_t, preferred_element_type=F32)
            act = _gelu(ht).astype(BF16)
            gates = []
            for e in range((base + c0) // N_KEYS, (base + c0 + EXPERT_CHUNK) // N_KEYS):
                gate = None
                for h in range(heads):
                    c = cnt1_ref[h, e:e + 1, :].astype(BF16)
                    a = a1_ref[h, e:e + 1, :].astype(BF16)
                    wa = w2_ref[h] * a
                    term = jnp.where(r2_ref[h] < c, wa, jnp.zeros_like(wa))
                    gate = term if gate is None else gate + term
                gates.append(gate)
            p_scr[c0:c0 + EXPERT_CHUNK, :] = jnp.concatenate(gates, axis=0) * act

    @pl.when(j == 0)
    def _():
        acc_scr[...] = jnp.zeros_like(acc_scr)
        pb_scr[...] = jnp.zeros_like(pb_scr)

    @pl.when(j < n_pair)
    def _():
        acc_scr[...] += jnp.dot(vt_prev_ref[...], pb_scr[...], preferred_element_type=F32)
        build(pa_scr, 0)
        acc_scr[...] += jnp.dot(vt_ref[...], pa_scr[...], preferred_element_type=F32)
        build(pb_scr, eb)

    @pl.when(j == n_pair)
    def _():
        acc = acc_scr[...] + jnp.dot(vt_prev_ref[...], pb_scr[...], preferred_element_type=F32)
        o_ref[...] = x_ref[...] + acc.T


def peer_experts(x, hn_t, u, vt, r2, cnt1, a1, w2, tn=512, eb=1024):
    t, d = x.shape
    ne = u.shape[0]
    heads = r2.shape[0]
    tn = min(tn, t)
    n_pair = ne // (2 * eb)
    n_grp = 2 * eb // N_KEYS
    last = n_pair - 1
    full = pl.BlockSpec((heads, N_KEYS, tn), lambda i, j: (0, 0, i))
    part = pl.BlockSpec((heads, n_grp, tn), lambda i, j: (0, jnp.minimum(j, last), i))
    return pl.pallas_call(
        _peer_expert_kernel,
        out_shape=jax.ShapeDtypeStruct((t, d), F32),
        grid=(t // tn, n_pair + 1),
        in_specs=[pl.BlockSpec((tn, d), lambda i, j: (i, 0)),
                  pl.BlockSpec((d, tn), lambda i, j: (0, i)),
                  pl.BlockSpec((2 * eb, d), lambda i, j: (jnp.minimum(j, last), 0)),
                  pl.BlockSpec((d, eb), lambda i, j: (0, jnp.maximum(2 * j - 1, 0))),
                  pl.BlockSpec((d, eb), lambda i, j: (0, jnp.minimum(2 * j, 2 * last))),
                  full, part, part, full],
        out_specs=pl.BlockSpec((tn, d), lambda i, j: (i, 0)),
        scratch_shapes=[pltpu.VMEM((d, tn), F32), pltpu.VMEM((eb, tn), BF16), pltpu.VMEM((eb, tn), BF16)],
        compiler_params=_params("parallel", "arbitrary"),
        name="peer_experts",
    )(x, hn_t, u, vt, vt, r2, cnt1, a1, w2)


def peer_ffn_residual(x, g, w_query, keys1, keys2, u_table, v_table):
    wq_t = w_query.T.astype(BF16)
    hn_t, r2, cnt1, a1, w2 = peer_route(x, g, wq_t, keys1, keys2)
    return peer_experts(x, hn_t, u_table.astype(BF16), v_table.T.astype(BF16), r2, cnt1, a1, w2)


def kernel(x_prompt, x_sample, cache_k_sb, cache_v_sb, page_table, state_conv_sc, state_conv_rg, state_h_rg, norm_mix, norm_ffn, norm_final, w_qkv_sb, w_o_sb, b_sb, w_in_sc, conv_w_sc, w_out_sc, w_in_rg, conv_w_rg, conv_b_rg, w_ra_rg, b_ra_rg, w_ri_rg, b_ri_rg, lam_rg, w_out_rg, w_query_peer, keys1_peer, keys2_peer, u_peer, v_peer):
    bp, seq, d = x_prompt.shape
    bs, dseq, _ = x_sample.shape
    n_p = bp * seq
    depth = norm_mix.shape[0]
    n_pool = cache_k_sb.shape[1]
    n_pages = page_table.shape[1]
    x = jnp.concatenate([x_prompt.reshape(n_p, d), x_sample.reshape(bs * dseq, d)], axis=0)
    kcache = cache_k_sb.reshape((-1,) + cache_k_sb.shape[2:])
    vcache = cache_v_sb.reshape((-1,) + cache_v_sb.shape[2:])
    k_p, v_p, k_s, v_s = [], [], [], []
    csc_p, csc_s, crg_p, crg_s, h_p, h_s = [], [], [], [], [], []
    for layer in range(depth):
        kind, j = layer % 3, layer // 3
        g_mix = norm_mix[layer][None, :]
        if kind == 0:
            qkv = norm_matmul(x, g_mix, w_qkv_sb[j].astype(BF16))
            o_p = sb_prompt(qkv, b_sb[j], bp, seq)
            page_ids = (page_table + j * n_pool).reshape(-1)
            bias_lanes = jnp.repeat(b_sb[j], dseq)[None, :]
            o_s = sb_sample(qkv, n_p, bs, dseq, page_ids, bias_lanes, kcache, vcache)
            mix = jnp.concatenate([o_p, o_s], axis=0)
            w_out = w_o_sb[j]
            k_p.append(qkv[:n_p, d:2 * d].reshape(bp, seq, N_HEADS, HEAD_DIM))
            v_p.append(qkv[:n_p, 2 * d:].reshape(bp, seq, N_HEADS, HEAD_DIM))
            k_s.append(qkv[n_p:, d:2 * d].reshape(bs, dseq, N_HEADS, HEAD_DIM))
            v_s.append(qkv[n_p:, 2 * d:].reshape(bs, dseq, N_HEADS, HEAD_DIM))
        elif kind == 1:
            proj = norm_matmul(x, g_mix, w_in_sc[j].astype(BF16))
            c = conv_w_sc.shape[2]
            g_p, nb_p = short_conv(proj, 0, bp, seq, conv_w_sc[j], jnp.zeros((bp, 2, c), F32))
            g_s, nb_s = short_conv(proj, n_p, bs, dseq, conv_w_sc[j], state_conv_sc[j])
            mix = jnp.concatenate([g_p, g_s], axis=0)
            w_out = w_out_sc[j]
            csc_p.append(nb_p)
            csc_s.append(nb_s)
        else:
            proj = norm_matmul(x, g_mix, w_in_rg[j].astype(BF16))
            c = conv_w_rg.shape[2]
            wts = (conv_w_rg[j], conv_b_rg[j][None, :], w_ra_rg[j].astype(BF16), b_ra_rg[j][None, :],
                   w_ri_rg[j].astype(BF16), b_ri_rg[j][None, :], lam_rg[j][None, :])
            g_p, nb_p, hl_p = rglru(proj, 0, bp, seq, RG_BLOCK, *wts,
                                    jnp.zeros((bp, 3, c), F32), jnp.zeros((bp, 1, c), F32))
            g_s, nb_s, hl_s = rglru(proj, n_p, bs, dseq, c, *wts, state_conv_rg[j], state_h_rg[j][:, None, :])
            mix = jnp.concatenate([g_p, g_s], axis=0)
            w_out = w_out_rg[j]
            crg_p.append(nb_p)
            crg_s.append(nb_s)
            h_p.append(hl_p[:, 0, :])
            h_s.append(hl_s[:, 0, :])
        x = matmul_residual(mix, w_out.astype(BF16), x)
        x = peer_ffn_residual(x, norm_ffn[layer][None, :], w_query_peer[layer], keys1_peer[layer],
                              keys2_peer[layer], u_peer[layer], v_peer[layer])
    y = final_norm(x, norm_final[None, :])
    return (y[:n_p].reshape(bp, seq, d), y[n_p:].reshape(bs, dseq, d),
            jnp.stack(k_p), jnp.stack(v_p), jnp.stack(csc_p), jnp.stack(crg_p), jnp.stack(h_p),
            jnp.stack(k_s), jnp.stack(v_s), jnp.stack(csc_s), jnp.stack(crg_s), jnp.stack(h_s))
```

```python
import functools
import math

import jax
import jax.numpy as jnp
from jax import lax
from jax.experimental import pallas as pl
from jax.experimental.pallas import tpu as pltpu

F32 = jnp.float32
BF16 = jnp.bfloat16
EPS = 1e-6
HEAD_DIM = 64
N_HEADS = 16
PAGE = 128
K_TILE = 128
OFF_DIAG_UNROLL = 2
RG_BLOCK = 128
RG_C = 8.0
TOPK = 16
N_KEYS = 128
EXPERT_CHUNK = 256
LANES = 128
SUBLANES = 8
VMEM_LIMIT = 48 * 1024 * 1024

NT_DIMS = (((1,), (1,)), ((), ()))
TN_DIMS = (((0,), (0,)), ((), ()))


def _params(*sem):
    return pltpu.CompilerParams(dimension_semantics=sem, vmem_limit_bytes=VMEM_LIMIT)


def _softplus(z):
    return jnp.maximum(z, 0.0) + jnp.log(1.0 + jnp.exp(-jnp.abs(z)))


def _sigmoid(z):
    return 1.0 / (1.0 + jnp.exp(-z))


def _gelu(x):
    return 0.5 * x * (1.0 + lax.erf(x * (1.0 / math.sqrt(2.0))))


def _rms(x, g):
    ms = jnp.mean(x * x, axis=-1, keepdims=True)
    return x * lax.rsqrt(ms + EPS) * g


def _split_bf16(x):
    hi = x.astype(BF16)
    lo = (x - hi.astype(F32)).astype(BF16)
    return hi, lo


def _norm_mm_kernel(x_ref, g_ref, w_ref, o_ref, hn_ref):
    @pl.when(pl.program_id(1) == 0)
    def _():
        hn_ref[...] = _rms(x_ref[...], g_ref[...]).astype(BF16)

    o_ref[...] = jnp.dot(hn_ref[...], w_ref[...], preferred_element_type=F32)


def norm_matmul(x, g, w, tm=512, tn=1024):
    t, d = x.shape
    n = w.shape[1]
    tm = min(tm, t)
    tn = min(tn, n)
    return pl.pallas_call(
        _norm_mm_kernel,
        out_shape=jax.ShapeDtypeStruct((t, n), F32),
        grid=(t // tm, n // tn),
        in_specs=[pl.BlockSpec((tm, d), lambda i, j: (i, 0)),
                  pl.BlockSpec((1, d), lambda i, j: (0, 0)),
                  pl.BlockSpec((d, tn), lambda i, j: (0, j))],
        out_specs=pl.BlockSpec((tm, tn), lambda i, j: (i, j)),
        scratch_shapes=[pltpu.VMEM((tm, d), BF16)],
        compiler_params=_params("parallel", "arbitrary"),
        name="norm_matmul",
    )(x, g, w)


def _mm_res_kernel(a_ref, w_ref, r_ref, o_ref):
    o_ref[...] = r_ref[...] + jnp.dot(a_ref[...].astype(BF16), w_ref[...], preferred_element_type=F32)


def matmul_residual(a, w, res, tm=512):
    t, k = a.shape
    n = w.shape[1]
    tm = min(tm, t)
    return pl.pallas_call(
        _mm_res_kernel,
        out_shape=jax.ShapeDtypeStruct((t, n), F32),
        grid=(t // tm,),
        in_specs=[pl.BlockSpec((tm, k), lambda i: (i, 0)),
                  pl.BlockSpec((k, n), lambda i: (0, 0)),
                  pl.BlockSpec((tm, n), lambda i: (i, 0))],
        out_specs=pl.BlockSpec((tm, n), lambda i: (i, 0)),
        compiler_params=_params("parallel"),
        name="matmul_residual",
    )(a, w, res)


def _final_norm_kernel(x_ref, g_ref, o_ref):
    o_ref[...] = _rms(x_ref[...], g_ref[...])


def final_norm(x, g, tm=512):
    t, d = x.shape
    tm = min(tm, t)
    return pl.pallas_call(
        _final_norm_kernel,
        out_shape=jax.ShapeDtypeStruct((t, d), F32),
        grid=(t // tm,),
        in_specs=[pl.BlockSpec((tm, d), lambda i: (i, 0)), pl.BlockSpec((1, d), lambda i: (0, 0))],
        out_specs=pl.BlockSpec((tm, d), lambda i: (i, 0)),
        compiler_params=_params("parallel"),
        name="final_norm",
    )(x, g)


def _sb_prompt_kernel(bias_ref, q_ref, k_ref, v_ref, o_ref, acc_scr, cs_scr):
    hp = pl.program_id(1)
    qi = pl.program_id(2)
    tq = q_ref.shape[0]
    row = lax.broadcasted_iota(jnp.int32, (2 * tq, K_TILE), 0)
    col = lax.broadcasted_iota(jnp.int32, (2 * tq, K_TILE), 1)
    head1 = row >= tq
    q_pos = jnp.where(head1, row - tq, row) + qi * tq
    tri_r = lax.broadcasted_iota(jnp.int32, (2 * K_TILE, K_TILE), 0) & (K_TILE - 1)
    tri_c = lax.broadcasted_iota(jnp.int32, (2 * K_TILE, K_TILE), 1)
    suffix2 = jnp.where(tri_r >= tri_c, 1.0, 0.0).astype(BF16)
    first = lax.broadcasted_iota(jnp.int32, (tq, 2 * HEAD_DIM), 1) < HEAD_DIM
    q = q_ref[...] * (1.0 / math.sqrt(HEAD_DIM))
    q2 = jnp.concatenate([jnp.where(first, q, 0.0), jnp.where(first, 0.0, q)], axis=0).astype(BF16)
    bias = jnp.where(head1, bias_ref[hp * 2 + 1], bias_ref[hp * 2])
    n_diag = tq // K_TILE
    n_kb = (qi + 1) * n_diag

    def tile(kb, cs, masked):
        rows = pl.ds(pl.multiple_of(kb * K_TILE, K_TILE), K_TILE)
        k = k_ref[rows, :].astype(BF16)
        v = v_ref[rows, :].astype(BF16)
        z = lax.dot_general(q2, k, NT_DIMS, preferred_element_type=F32) + bias
        ln = -_softplus(z)
        if masked:
            mask = (col + kb * K_TILE) < q_pos
            ln = jnp.where(mask, ln, 0.0)
        hi, lo = _split_bf16(ln)
        part = jnp.dot(jnp.concatenate([hi, lo], axis=1), suffix2, preferred_element_type=F32)
        w = jnp.exp(z + (part + cs))
        if masked:
            w = jnp.where(mask, w, 0.0)
        return jnp.dot(w.astype(BF16), v, preferred_element_type=F32), cs + part[:, 0:1]

    cs = jnp.zeros((2 * tq, 1), F32)
    acc = None
    for t in range(n_diag):
        pv, cs = tile(n_kb - 1 - t, cs, True)
        acc = pv if acc is None else acc + pv
    acc_scr[...] = acc
    cs_scr[...] = cs

    def body(i, _):
        kb = n_kb - n_diag - 1 - OFF_DIAG_UNROLL * i
        cs = cs_scr[...]
        total = None
        for t in range(OFF_DIAG_UNROLL):
            pv, cs = tile(kb - t, cs, False)
            total = pv if total is None else total + pv
        acc_scr[...] += total
        cs_scr[...] = cs
        return 0

    lax.fori_loop(0, (n_kb - n_diag) // OFF_DIAG_UNROLL, body, 0)
    o_ref[...] = jnp.where(first, acc_scr[0:tq, :], acc_scr[tq:2 * tq, :])


def sb_prompt(qkv, bias, n_batch, seq, tq=256):
    d = qkv.shape[1] // 3
    n_pair = d // (2 * HEAD_DIM)
    tq = min(tq, seq)
    assert (tq // K_TILE) % OFF_DIAG_UNROLL == 0
    n_q = seq // tq
    return pl.pallas_call(
        _sb_prompt_kernel,
        out_shape=jax.ShapeDtypeStruct((n_batch * seq, d), F32),
        grid=(n_batch, n_pair, n_q),
        in_specs=[pl.BlockSpec(memory_space=pltpu.SMEM),
                  pl.BlockSpec((tq, 2 * HEAD_DIM), lambda b, p, i: (b * n_q + i, p)),
                  pl.BlockSpec((seq, 2 * HEAD_DIM), lambda b, p, i: (b, n_pair + p)),
                  pl.BlockSpec((seq, 2 * HEAD_DIM), lambda b, p, i: (b, 2 * n_pair + p))],
        out_specs=pl.BlockSpec((tq, 2 * HEAD_DIM), lambda b, p, i: (b * n_q + i, p)),
        scratch_shapes=[pltpu.VMEM((2 * tq, 2 * HEAD_DIM), F32), pltpu.VMEM((2 * tq, 1), F32)],
        compiler_params=_params("parallel", "parallel", "arbitrary"),
        name="sb_prompt",
    )(bias, qkv, qkv, qkv)


def _sb_sample_kernel(pt_ref, q_ref, kn_ref, vn_ref, bias_ref, *rest):
    del pt_ref
    pps = (len(rest) - 6) // 2
    kc_refs, vc_refs = rest[:pps], rest[pps:2 * pps]
    o_ref, qrep_ref, kpad_ref, vpad_ref, acc_ref, cs_ref = rest[2 * pps:]
    s = pl.program_id(1)
    n_q = q_ref.shape[0]
    d = q_ref.shape[1]
    row = lax.broadcasted_iota(jnp.int32, (LANES, PAGE), 0)
    lane = lax.broadcasted_iota(jnp.int32, (LANES, PAGE), 1)
    tri_r = lax.broadcasted_iota(jnp.int32, (2 * PAGE, PAGE), 0) & (PAGE - 1)
    tri_c = lax.broadcasted_iota(jnp.int32, (2 * PAGE, PAGE), 1)
    suffix2 = jnp.where(tri_r >= tri_c, 1.0, 0.0).astype(BF16)

    def process(kblks, vblks, mask, paged):
        zs, parts = [], []
        for kblk in kblks:
            if paged:
                z = jnp.dot(qrep_ref[...], kblk, preferred_element_type=F32)
            else:
                z = lax.dot_general(qrep_ref[...], kblk, NT_DIMS, preferred_element_type=F32)
            z = z + bias_ref[...]
            ln = -_softplus(z)
            if mask is not None:
                ln = jnp.where(mask, ln, 0.0)
            hi, lo = _split_bf16(ln)
            parts.append(jnp.dot(jnp.concatenate([hi, lo], axis=1), suffix2, preferred_element_type=F32))
            zs.append(z)
        carry = cs_ref[...]
        total = None
        for z, part, vblk in zip(zs, parts, vblks):
            w = jnp.exp(z + (part + carry))
            if mask is not None:
                w = jnp.where(mask, w, 0.0)
            if paged:
                pv = lax.dot_general(w.astype(BF16), vblk, NT_DIMS, preferred_element_type=F32)
            else:
                pv = jnp.dot(w.astype(BF16), vblk, preferred_element_type=F32)
            total = pv if total is None else total + pv
            carry = carry + part[:, 0:1]
        acc_ref[...] += total
        cs_ref[...] = carry

    @pl.when(s == 0)
    def _():
        q = q_ref[...] * (1.0 / math.sqrt(HEAD_DIM))
        qrep = jnp.concatenate([q] * N_HEADS, axis=0)
        r_head = lax.broadcasted_iota(jnp.int32, (N_HEADS * n_q, d), 0) >> (n_q.bit_length() - 1)
        c_head = lax.broadcasted_iota(jnp.int32, (N_HEADS * n_q, d), 1) >> (HEAD_DIM.bit_length() - 1)
        qrep_ref[...] = jnp.where(r_head == c_head, qrep, 0.0).astype(BF16)
        acc_ref[...] = jnp.zeros_like(acc_ref)
        cs_ref[...] = jnp.zeros_like(cs_ref)
        kpad_ref[...] = jnp.zeros_like(kpad_ref)
        vpad_ref[...] = jnp.zeros_like(vpad_ref)
        kpad_ref[0:n_q, :] = kn_ref[...]
        vpad_ref[0:n_q, :] = vn_ref[...]
        mask = lane < (row & (n_q - 1))
        process([kpad_ref[...].astype(BF16)], [vpad_ref[...].astype(BF16)], mask, paged=False)

    @pl.when(s > 0)
    def _():
        process([r[...].reshape(d, PAGE).astype(BF16) for r in kc_refs],
                [r[...].reshape(d, PAGE).astype(BF16) for r in vc_refs], None, paged=True)

    @pl.when(s == pl.num_programs(1) - 1)
    def _():
        half = lax.broadcasted_iota(jnp.int32, (n_q, LANES), 1) < HEAD_DIM
        outs = []
        for j in range(d // LANES):
            a = acc_ref[2 * j * n_q:(2 * j + 1) * n_q, j * LANES:(j + 1) * LANES]
            b = acc_ref[(2 * j + 1) * n_q:(2 * j + 2) * n_q, j * LANES:(j + 1) * LANES]
            outs.append(jnp.where(half, a, b))
        o_ref[...] = jnp.concatenate(outs, axis=-1)


def sb_sample(qkv, row0, n_batch, n_q, page_ids, bias_rows, kcache, vcache, pps=4):
    d = qkv.shape[1] // 3
    n_pages = page_ids.shape[0] // n_batch
    pps = math.gcd(pps, n_pages)
    assert N_HEADS * n_q == LANES and n_q == SUBLANES and row0 % n_q == 0
    rb = row0 // n_q

    def page_spec(r):
        def page_map(b, s, pt):
            return (pt[b * n_pages + n_pages - pps * jnp.maximum(s - 1, 0) - 1 - r], 0, 0, 0)
        return pl.BlockSpec((None, N_HEADS, HEAD_DIM, PAGE), page_map)

    grid_spec = pltpu.PrefetchScalarGridSpec(
        num_scalar_prefetch=1,
        grid=(n_batch, n_pages // pps + 1),
        in_specs=[pl.BlockSpec((n_q, d), lambda b, s, pt: (rb + b, 0)),
                  pl.BlockSpec((n_q, d), lambda b, s, pt: (rb + b, 1)),
                  pl.BlockSpec((n_q, d), lambda b, s, pt: (rb + b, 2)),
                  pl.BlockSpec((LANES, PAGE), lambda b, s, pt: (0, 0))]
                 + [page_spec(r) for r in range(pps)] + [page_spec(r) for r in range(pps)],
        out_specs=pl.BlockSpec((n_q, d), lambda b, s, pt: (b, 0)),
        scratch_shapes=[pltpu.VMEM((LANES, d), BF16),
                        pltpu.VMEM((PAGE, d), F32),
                        pltpu.VMEM((PAGE, d), F32),
                        pltpu.VMEM((LANES, d), F32),
                        pltpu.VMEM((LANES, 1), F32)],
    )
    return pl.pallas_call(
        _sb_sample_kernel,
        out_shape=jax.ShapeDtypeStruct((n_batch * n_q, d), F32),
        grid_spec=grid_spec,
        compiler_params=_params("parallel", "arbitrary"),
        name="sb_sample",
    )(page_ids, qkv, qkv, qkv, bias_rows, *([kcache] * pps), *([vcache] * pps))


def _shifted(u, k, buf_ref, row):
    nb = buf_ref.shape[0]
    s = pltpu.roll(u, k, axis=0)
    for r in range(k):
        s = jnp.where(row == r, buf_ref[nb - k + r:nb - k + r + 1, :], s)
    return s


def _short_conv_kernel(bg_ref, cg_ref, xv_ref, w_ref, buf_ref, g_ref, nb_ref):
    u = cg_ref[...] * xv_ref[...]
    t = u.shape[0]
    row = lax.broadcasted_iota(jnp.int32, u.shape, 0)
    y = _shifted(u, 2, buf_ref, row) * w_ref[0:1, :]
    y = y + _shifted(u, 1, buf_ref, row) * w_ref[1:2, :]
    y = y + u * w_ref[2:3, :]
    g_ref[...] = bg_ref[...] * y
    nb_ref[...] = u[t - 2:t, :]


def short_conv(proj, row0, n_batch, seq, conv_w, buf, tc=512):
    c = proj.shape[1] // 3
    nc = c // tc
    rb = row0 // seq
    return pl.pallas_call(
        _short_conv_kernel,
        out_shape=(jax.ShapeDtypeStruct((n_batch * seq, c), F32),
                   jax.ShapeDtypeStruct((n_batch, 2, c), F32)),
        grid=(n_batch, nc),
        in_specs=[pl.BlockSpec((seq, tc), lambda b, j: (rb + b, j)),
                  pl.BlockSpec((seq, tc), lambda b, j: (rb + b, nc + j)),
                  pl.BlockSpec((seq, tc), lambda b, j: (rb + b, 2 * nc + j)),
                  pl.BlockSpec((3, tc), lambda b, j: (0, j)),
                  pl.BlockSpec((None, 2, tc), lambda b, j: (b, 0, j))],
        out_specs=(pl.BlockSpec((seq, tc), lambda b, j: (b, j)),
                   pl.BlockSpec((None, 2, tc), lambda b, j: (b, 0, j))),
        compiler_params=_params("parallel", "parallel"),
        name="short_conv",
    )(proj, proj, proj, conv_w, buf)


def _scan_tile(a, d, row):
    for k in (1, 2, 4):
        a_s = pltpu.roll(a, k, axis=0)
        d_s = pltpu.roll(d, k, axis=0)
        valid = row >= k
        d = jnp.where(valid, a * d_s + d, d)
        a = jnp.where(valid, a * a_s, a)
    return a, d


def _rglru_kernel(gate_ref, u_ref, cw_ref, cb_ref, wra_ref, bra_ref, wri_ref, bri_ref, lam_ref,
                  buf_ref, h0_ref, o_ref, nb_ref, hl_ref, a_scr, d_scr):
    up = u_ref[...]
    t, c = up.shape
    row = lax.broadcasted_iota(jnp.int32, up.shape, 0)
    y = _shifted(up, 3, buf_ref, row) * cw_ref[0:1, :]
    y = y + _shifted(up, 2, buf_ref, row) * cw_ref[1:2, :]
    y = y + _shifted(up, 1, buf_ref, row) * cw_ref[2:3, :]
    y = y + up * cw_ref[3:4, :]
    u = y + cb_ref[...]
    rs, igs = [], []
    for n in range(c // RG_BLOCK):
        ub = u[:, n * RG_BLOCK:(n + 1) * RG_BLOCK].astype(BF16)
        rs.append(jnp.dot(ub, wra_ref[n], preferred_element_type=F32))
        igs.append(jnp.dot(ub, wri_ref[n], preferred_element_type=F32))
    r = _sigmoid(jnp.concatenate(rs, axis=-1) + bra_ref[...])
    ig = _sigmoid(jnp.concatenate(igs, axis=-1) + bri_ref[...])
    log_a = -RG_C * r * _softplus(-lam_ref[...])
    a = jnp.exp(log_a)
    drive = jnp.sqrt(-jnp.tanh(log_a) * (a * a + 1.0)) * (ig * u)
    a_scr[...] = a
    d_scr[...] = drive
    row8 = lax.broadcasted_iota(jnp.int32, (SUBLANES, c), 0)

    def body(i, h):
        rows = pl.ds(pl.multiple_of(i * SUBLANES, SUBLANES), SUBLANES)
        a_c, d_c = _scan_tile(a_scr[rows, :], d_scr[rows, :], row8)
        hs = a_c * h + d_c
        d_scr[rows, :] = hs
        return hs[SUBLANES - 1:SUBLANES, :]

    h_last = lax.fori_loop(0, t // SUBLANES, body, h0_ref[...])
    o_ref[...] = _gelu(gate_ref[...]) * d_scr[...]
    nb_ref[...] = up[t - 3:t, :]
    hl_ref[...] = h_last


def rglru(proj, row0, n_batch, seq, tc, conv_w, conv_b, w_ra, b_ra, w_ri, b_ri, lam, buf, h0):
    c = proj.shape[1] // 2
    nc = c // tc
    nblk = tc // RG_BLOCK
    rb = row0 // seq
    vec = pl.BlockSpec((1, tc), lambda b, j: (0, j))
    return pl.pallas_call(
        _rglru_kernel,
        out_shape=(jax.ShapeDtypeStruct((n_batch * seq, c), F32),
                   jax.ShapeDtypeStruct((n_batch, 3, c), F32),
                   jax.ShapeDtypeStruct((n_batch, 1, c), F32)),
        grid=(n_batch, nc),
        in_specs=[pl.BlockSpec((seq, tc), lambda b, j: (rb + b, j)),
                  pl.BlockSpec((seq, tc), lambda b, j: (rb + b, nc + j)),
                  pl.BlockSpec((4, tc), lambda b, j: (0, j)),
                  vec,
                  pl.BlockSpec((nblk, RG_BLOCK, RG_BLOCK), lambda b, j: (j, 0, 0)),
                  vec,
                  pl.BlockSpec((nblk, RG_BLOCK, RG_BLOCK), lambda b, j: (j, 0, 0)),
                  vec,
                  vec,
                  pl.BlockSpec((None, 3, tc), lambda b, j: (b, 0, j)),
                  pl.BlockSpec((None, 1, tc), lambda b, j: (b, 0, j))],
        out_specs=(pl.BlockSpec((seq, tc), lambda b, j: (b, j)),
                   pl.BlockSpec((None, 3, tc), lambda b, j: (b, 0, j)),
                   pl.BlockSpec((None, 1, tc), lambda b, j: (b, 0, j))),
        scratch_shapes=[pltpu.VMEM((seq, tc), F32), pltpu.VMEM((seq, tc), F32)],
        compiler_params=_params("parallel", "parallel"),
        name="rglru",
    )(proj, proj, conv_w, conv_b, w_ra, b_ra, w_ri, b_ri, lam, buf, h0)


def _top16(*scores):
    lanes = scores[0].shape[1]
    io_k = lax.broadcasted_iota(jnp.int32, (TOPK, lanes), 0)

    def body(i, carry):
        new = []
        for s, rank, vals in carry:
            n = s.shape[0]
            io = lax.broadcasted_iota(jnp.int32, s.shape, 0)
            m = jnp.max(s, axis=0, keepdims=True)
            idx = jnp.min(jnp.where(s == m, io, n), axis=0, keepdims=True)
            sel = io == idx
            new.append((jnp.where(sel, -jnp.inf, s), jnp.where(sel, i, rank), jnp.where(io_k == i, m, vals)))
        return tuple(new)

    init = tuple((s, jnp.full(s.shape, TOPK, jnp.int32), jnp.zeros((TOPK, lanes), F32)) for s in scores)
    return [(rank, vals) for _, rank, vals in lax.fori_loop(0, TOPK, body, init)]


def _top16_distinct(*scores):
    lanes = scores[0].shape[1]
    io_k = lax.broadcasted_iota(jnp.int32, (TOPK, lanes), 0)

    def body(i, carry):
        new = []
        for s, vals in carry:
            m = jnp.max(s, axis=0, keepdims=True)
            new.append((jnp.where(s == m, -jnp.inf, s), jnp.where(io_k == i, m, vals)))
        return tuple(new)

    init = tuple((s, jnp.zeros((TOPK, lanes), F32)) for s in scores)
    return list(lax.fori_loop(0, TOPK, body, init))


def _removed_count(s_after):
    return jnp.sum(jnp.where(s_after == -jnp.inf, 1.0, 0.0), axis=0, keepdims=True)


def _staircase(v1, v2):
    io8 = lax.broadcasted_iota(jnp.int32, (SUBLANES, v1.shape[1]), 0)
    pad = float(jnp.finfo(F32).min)
    parts = [v1[0:1, :] + v2, v1[1:2, :] + v2[0:SUBLANES, :]]
    for i in range(2, SUBLANES):
        parts.append(jnp.where(io8 < TOPK // (i + 1), v1[i:i + 1, :] + v2[0:SUBLANES, :], pad))
    parts.append(v1[SUBLANES:TOPK, :] + v2[0:1, :])
    return jnp.concatenate(parts, axis=0)


def _staircase_counts(sel):
    rows = [jnp.sum(sel[0:TOPK, :], axis=0, keepdims=True)]
    for i in range(1, SUBLANES):
        lo = TOPK + SUBLANES * (i - 1)
        rows.append(jnp.sum(sel[lo:lo + SUBLANES, :], axis=0, keepdims=True))
    lo = TOPK + SUBLANES * (SUBLANES - 1)
    rows += [sel[lo + r:lo + r + 1, :] for r in range(SUBLANES)]
    return rows


def _peer_route_kernel(x_ref, g_ref, wq_ref, k1_ref, k2_ref, hnt_ref, r2_ref, cnt1_ref, a1_ref, w2_ref, qt_scr):
    h = pl.program_id(1)
    tn = x_ref.shape[0]

    @pl.when(h == 0)
    def _():
        hn_t = _rms(x_ref[...], g_ref[...]).T.astype(BF16)
        hnt_ref[...] = hn_t
        qt_scr[...] = jnp.dot(wq_ref[...], hn_t, preferred_element_type=F32)

    off = pl.multiple_of(h * 2 * N_KEYS, 2 * N_KEYS)
    s1_all = jnp.dot(k1_ref[...], qt_scr[pl.ds(off, N_KEYS), :],
                     precision=lax.Precision.HIGHEST, preferred_element_type=F32)
    s2_all = jnp.dot(k2_ref[...], qt_scr[pl.ds(off + N_KEYS, N_KEYS), :],
                     precision=lax.Precision.HIGHEST, preferred_element_type=F32)

    def route(s1, s2, sl, exact):
        if exact:
            (rank1, v1), (rank2, v2) = _top16(s1, s2)
            r2 = rank2.astype(F32)
        else:
            (left1, v1), (left2, v2) = _top16_distinct(s1, s2)
            r2 = jnp.zeros((N_KEYS, LANES), F32)
            for i in range(TOPK):
                r2 = r2 + jnp.where(v2[i:i + 1, :] > s2, 1.0, 0.0)
            r2 = jnp.where(left2 == -jnp.inf, r2, float(TOPK))
        cand = _staircase(v1, v2)
        if exact:
            ((rank_c, _),) = _top16(cand)
            sel = rank_c < TOPK
        else:
            ((left_c, _),) = _top16_distinct(cand)
            sel = left_c == -jnp.inf
        cnt = _staircase_counts(jnp.where(sel, 1.0, 0.0))
        z = jnp.sum(jnp.where(sel, jnp.exp(cand - cand[0:1, :]), 0.0), axis=0, keepdims=True)
        cnt1 = jnp.zeros((N_KEYS, LANES), F32)
        for i in range(TOPK):
            cnt1 = jnp.where((rank1 == i) if exact else (s1 == v1[i:i + 1, :]), cnt[i], cnt1)
        r2_ref[:, sl] = r2.astype(BF16)
        cnt1_ref[:, sl] = cnt1
        a1_ref[:, sl] = jnp.exp(s1 - v1[0:1, :])
        w2_ref[:, sl] = (jnp.exp(s2 - v2[0:1, :]) / z).astype(BF16)
        if exact:
            return None
        return (jnp.abs(_removed_count(left1) - TOPK) + jnp.abs(_removed_count(left2) - TOPK)
                + jnp.abs(_removed_count(left_c) - TOPK))

    groups = [slice(lg * LANES, (lg + 1) * LANES) for lg in range(tn // LANES)]
    ties = None
    for sl in groups:
        off16 = route(s1_all[:, sl], s2_all[:, sl], sl, exact=False)
        ties = off16 if ties is None else ties + off16

    @pl.when(jnp.max(ties) > 0.0)
    def _():
        for sl in groups:
            route(s1_all[:, sl], s2_all[:, sl], sl, exact=True)


def peer_route(x, g, wq_t, keys1, keys2, tn=512):
    t, d = x.shape
    heads = keys1.shape[0]
    tn = min(tn, t)
    side = jax.ShapeDtypeStruct((heads, N_KEYS, t), F32)
    side_bf = jax.ShapeDtypeStruct((heads, N_KEYS, t), BF16)
    side_spec = pl.BlockSpec((None, N_KEYS, tn), lambda i, h: (h, 0, i))
    return pl.pallas_call(
        _peer_route_kernel,
        out_shape=(jax.ShapeDtypeStruct((d, t), BF16), side_bf, side, side, side_bf),
        grid=(t // tn, heads),
        in_specs=[pl.BlockSpec((tn, d), lambda i, h: (i, 0)),
                  pl.BlockSpec((1, d), lambda i, h: (0, 0)),
                  pl.BlockSpec(wq_t.shape, lambda i, h: (0, 0)),
                  pl.BlockSpec((None, N_KEYS, N_KEYS), lambda i, h: (h, 0, 0)),
                  pl.BlockSpec((None, N_KEYS, N_KEYS), lambda i, h: (h, 0, 0))],
        out_specs=(pl.BlockSpec((d, tn), lambda i, h: (0, i)), side_spec, side_spec, side_spec, side_spec),
        scratch_shapes=[pltpu.VMEM((wq_t.shape[0], tn), F32)],
        compiler_params=_params("parallel", "arbitrary"),
        name="peer_route",
    )(x, g, wq_t, keys1, keys2)


def _peer_expert_kernel(x_ref, hnt_ref, u_ref, vt_prev_ref, vt_ref, r2_ref, cnt1_ref, a1_ref, w2_ref, o_ref,
                        acc_scr, pa_scr, pb_scr):
    j = pl.program_id(1)
    n_pair = pl.num_programs(1) - 1
    heads = r2_ref.shape[0]
    eb = u_ref.shape[0] // 2

    def build(p_scr, base):
        hn_t = hnt_ref[...]
        for c0 in range(0, eb, EXPERT_CHUNK):
            ht = jnp.dot(u_ref[base + c0:base + c0 + EXPERT_CHUNK, :], hn_t, preferred_element_type=F32)
            act = _gelu(ht).astype(BF16)
            gates = []
            for e in range((base + c0) // N_KEYS, (base + c0 + EXPERT_CHUNK) // N_KEYS):
                gate = None
                for h in range(heads):
                    c = cnt1_ref[h, e:e + 1, :].astype(BF16)
                    a = a1_ref[h, e:e + 1, :].astype(BF16)
                    wa = w2_ref[h] * a
                    term = jnp.where(r2_ref[h] < c, wa, jnp.zeros_like(wa))
                    gate = term if gate is None else gate + term
                gates.append(gate)
            p_scr[c0:c0 + EXPERT_CHUNK, :] = jnp.concatenate(gates, axis=0) * act

    @pl.when(j == 0)
    def _():
        acc_scr[...] = jnp.zeros_like(acc_scr)
        pb_scr[...] = jnp.zeros_like(pb_scr)

    @pl.when(j < n_pair)
    def _():
        acc_scr[...] += jnp.dot(vt_prev_ref[...], pb_scr[...], preferred_element_type=F32)
        build(pa_scr, 0)
        acc_scr[...] += jnp.dot(vt_ref[...], pa_scr[...], preferred_element_type=F32)
        build(pb_scr, eb)

    @pl.when(j == n_pair)
    def _():
        acc = acc_scr[...] + jnp.dot(vt_prev_ref[...], pb_scr[...], preferred_element_type=F32)
        o_ref[...] = x_ref[...] + acc.T


def peer_experts(x, hn_t, u, vt, r2, cnt1, a1, w2, tn=512, eb=1024):
    t, d = x.shape
    ne = u.shape[0]
    heads = r2.shape[0]
    tn = min(tn, t)
    n_pair = ne // (2 * eb)
    n_grp = 2 * eb // N_KEYS
    last = n_pair - 1
    full = pl.BlockSpec((heads, N_KEYS, tn), lambda i, j: (0, 0, i))
    part = pl.BlockSpec((heads, n_grp, tn), lambda i, j: (0, jnp.minimum(j, last), i))
    return pl.pallas_call(
        _peer_expert_kernel,
        out_shape=jax.ShapeDtypeStruct((t, d), F32),
        grid=(t // tn, n_pair + 1),
        in_specs=[pl.BlockSpec((tn, d), lambda i, j: (i, 0)),
                  pl.BlockSpec((d, tn), lambda i, j: (0, i)),
                  pl.BlockSpec((2 * eb, d), lambda i, j: (jnp.minimum(j, last), 0)),
                  pl.BlockSpec((d, eb), lambda i, j: (0, jnp.maximum(2 * j - 1, 0))),
                  pl.BlockSpec((d, eb), lambda i, j: (0, jnp.minimum(2 * j, 2 * last))),
                  full, part, part, full],
        out_specs=pl.BlockSpec((tn, d), lambda i, j: (i, 0)),
        scratch_shapes=[pltpu.VMEM((d, tn), F32), pltpu.VMEM((eb, tn), BF16), pltpu.VMEM((eb, tn), BF16)],
        compiler_params=_params("parallel", "arbitrary"),
        name="peer_experts",
    )(x, hn_t, u, vt, vt, r2, cnt1, a1, w2)


def peer_ffn_residual(x, g, w_query, keys1, keys2, u_table, v_table):
    wq_t = w_query.T.astype(BF16)
    hn_t, r2, cnt1, a1, w2 = peer_route(x, g, wq_t, keys1, keys2)
    return peer_experts(x, hn_t, u_table.astype(BF16), v_table.T.astype(BF16), r2, cnt1, a1, w2)


def kernel(x_prompt, x_sample, cache_k_sb, cache_v_sb, page_table, state_conv_sc, state_conv_rg, state_h_rg, norm_mix, norm_ffn, norm_final, w_qkv_sb, w_o_sb, b_sb, w_in_sc, conv_w_sc, w_out_sc, w_in_rg, conv_w_rg, conv_b_rg, w_ra_rg, b_ra_rg, w_ri_rg, b_ri_rg, lam_rg, w_out_rg, w_query_peer, keys1_peer, keys2_peer, u_peer, v_peer):
    bp, seq, d = x_prompt.shape
    bs, dseq, _ = x_sample.shape
    n_p = bp * seq
    depth = norm_mix.shape[0]
    n_pool = cache_k_sb.shape[1]
    n_pages = page_table.shape[1]
    x = jnp.concatenate([x_prompt.reshape(n_p, d), x_sample.reshape(bs * dseq, d)], axis=0)
    kcache = jnp.transpose(cache_k_sb, (0, 1, 3, 4, 2)).reshape(-1, N_HEADS, HEAD_DIM, PAGE)
    vcache = jnp.transpose(cache_v_sb, (0, 1, 3, 4, 2)).reshape(-1, N_HEADS, HEAD_DIM, PAGE)
    k_p, v_p, k_s, v_s = [], [], [], []
    csc_p, csc_s, crg_p, crg_s, h_p, h_s = [], [], [], [], [], []
    for layer in range(depth):
        kind, j = layer % 3, layer // 3
        g_mix = norm_mix[layer][None, :]
        if kind == 0:
            qkv = norm_matmul(x, g_mix, w_qkv_sb[j].astype(BF16))
            o_p = sb_prompt(qkv, b_sb[j], bp, seq)
            page_ids = (page_table + j * n_pool).reshape(-1)
            bias_rows = jnp.broadcast_to(jnp.repeat(b_sb[j], dseq)[:, None], (N_HEADS * dseq, PAGE))
            o_s = sb_sample(qkv, n_p, bs, dseq, page_ids, bias_rows, kcache, vcache)
            mix = jnp.concatenate([o_p, o_s], axis=0)
            w_out = w_o_sb[j]
            k_p.append(qkv[:n_p, d:2 * d].reshape(bp, seq, N_HEADS, HEAD_DIM))
            v_p.append(qkv[:n_p, 2 * d:].reshape(bp, seq, N_HEADS, HEAD_DIM))
            k_s.append(qkv[n_p:, d:2 * d].reshape(bs, dseq, N_HEADS, HEAD_DIM))
            v_s.append(qkv[n_p:, 2 * d:].reshape(bs, dseq, N_HEADS, HEAD_DIM))
        elif kind == 1:
            proj = norm_matmul(x, g_mix, w_in_sc[j].astype(BF16))
            c = conv_w_sc.shape[2]
            g_p, nb_p = short_conv(proj, 0, bp, seq, conv_w_sc[j], jnp.zeros((bp, 2, c), F32))
            g_s, nb_s = short_conv(proj, n_p, bs, dseq, conv_w_sc[j], state_conv_sc[j])
            mix = jnp.concatenate([g_p, g_s], axis=0)
            w_out = w_out_sc[j]
            csc_p.append(nb_p)
            csc_s.append(nb_s)
        else:
            proj = norm_matmul(x, g_mix, w_in_rg[j].astype(BF16))
            c = conv_w_rg.shape[2]
            wts = (conv_w_rg[j], conv_b_rg[j][None, :], w_ra_rg[j].astype(BF16), b_ra_rg[j][None, :],
                   w_ri_rg[j].astype(BF16), b_ri_rg[j][None, :], lam_rg[j][None, :])
            g_p, nb_p, hl_p = rglru(proj, 0, bp, seq, RG_BLOCK, *wts,
                                    jnp.zeros((bp, 3, c), F32), jnp.zeros((bp, 1, c), F32))
            g_s, nb_s, hl_s = rglru(proj, n_p, bs, dseq, c, *wts, state_conv_rg[j], state_h_rg[j][:, None, :])
            mix = jnp.concatenate([g_p, g_s], axis=0)
            w_out = w_out_rg[j]
            crg_p.append(nb_p)
            crg_s.append(nb_s)
            h_p.append(hl_p[:, 0, :])
            h_s.append(hl_s[:, 0, :])
        x = matmul_residual(mix, w_out.astype(BF16), x)
        x = peer_ffn_residual(x, norm_ffn[layer][None, :], w_query_peer[layer], keys1_peer[layer],
                              keys2_peer[layer], u_peer[layer], v_peer[layer])
    y = final_norm(x, norm_final[None, :])
    return (y[:n_p].reshape(bp, seq, d), y[n_p:].reshape(bs, dseq, d),
            jnp.stack(k_p), jnp.stack(v_p), jnp.stack(csc_p), jnp.stack(crg_p), jnp.stack(h_p),
            jnp.stack(k_s), jnp.stack(v_s), jnp.stack(csc_s), jnp.stack(crg_s), jnp.stack(h_s))
```

```python
import functools
import math

import jax
import jax.numpy as jnp
from jax import lax
from jax.experimental import pallas as pl
from jax.experimental.pallas import tpu as pltpu

F32 = jnp.float32
BF16 = jnp.bfloat16
EPS = 1e-6
HEAD_DIM = 64
N_HEADS = 16
PAGE = 128
K_TILE = 128
OFF_DIAG_UNROLL = 2
RG_BLOCK = 128
RG_C = 8.0
TOPK = 16
N_KEYS = 128
EXPERT_CHUNK = 256
LANES = 128
SUBLANES = 8
VMEM_LIMIT = 48 * 1024 * 1024

NT_DIMS = (((1,), (1,)), ((), ()))
TN_DIMS = (((0,), (0,)), ((), ()))


def _params(*sem):
    return pltpu.CompilerParams(dimension_semantics=sem, vmem_limit_bytes=VMEM_LIMIT)


def _softplus(z):
    return jnp.maximum(z, 0.0) + jnp.log(1.0 + jnp.exp(-jnp.abs(z)))


def _sigmoid(z):
    return 1.0 / (1.0 + jnp.exp(-z))


def _gelu(x):
    return 0.5 * x * (1.0 + lax.erf(x * (1.0 / math.sqrt(2.0))))


def _rms(x, g):
    ms = jnp.mean(x * x, axis=-1, keepdims=True)
    return x * lax.rsqrt(ms + EPS) * g


def _split_bf16(x):
    hi = x.astype(BF16)
    lo = (x - hi.astype(F32)).astype(BF16)
    return hi, lo


def _dot_3pass(a, b):
    a_hi, a_lo = _split_bf16(a)
    b_hi, b_lo = _split_bf16(b)
    return (jnp.dot(a_hi, b_hi, preferred_element_type=F32)
            + (jnp.dot(a_hi, b_lo, preferred_element_type=F32) + jnp.dot(a_lo, b_hi, preferred_element_type=F32)))


def _norm_mm_kernel(x_ref, g_ref, w_ref, o_ref, hn_ref):
    @pl.when(pl.program_id(1) == 0)
    def _():
        hn_ref[...] = _rms(x_ref[...], g_ref[...]).astype(BF16)

    o_ref[...] = jnp.dot(hn_ref[...], w_ref[...], preferred_element_type=F32)


def norm_matmul(x, g, w, tm=512, tn=1024):
    t, d = x.shape
    n = w.shape[1]
    tm = min(tm, t)
    tn = min(tn, n)
    return pl.pallas_call(
        _norm_mm_kernel,
        out_shape=jax.ShapeDtypeStruct((t, n), F32),
        grid=(t // tm, n // tn),
        in_specs=[pl.BlockSpec((tm, d), lambda i, j: (i, 0)),
                  pl.BlockSpec((1, d), lambda i, j: (0, 0)),
                  pl.BlockSpec((d, tn), lambda i, j: (0, j))],
        out_specs=pl.BlockSpec((tm, tn), lambda i, j: (i, j)),
        scratch_shapes=[pltpu.VMEM((tm, d), BF16)],
        compiler_params=_params("parallel", "arbitrary"),
        name="norm_matmul",
    )(x, g, w)


def _mm_res_kernel(n_first, a1_ref, a2_ref, w_ref, r_ref, o_ref):
    i = pl.program_id(0)

    @pl.when(i < n_first)
    def _():
        o_ref[...] = r_ref[...] + jnp.dot(a1_ref[...].astype(BF16), w_ref[...], preferred_element_type=F32)

    @pl.when(i >= n_first)
    def _():
        o_ref[...] = r_ref[...] + jnp.dot(a2_ref[...].astype(BF16), w_ref[...], preferred_element_type=F32)


def matmul_residual(a1, a2, w, res, tm=512):
    t1, k = a1.shape
    t2 = a2.shape[0]
    n = w.shape[1]
    tm = math.gcd(tm, math.gcd(t1, t2))
    n1, n2 = t1 // tm, t2 // tm
    return pl.pallas_call(
        functools.partial(_mm_res_kernel, n1),
        out_shape=jax.ShapeDtypeStruct((t1 + t2, n), F32),
        grid=(n1 + n2,),
        in_specs=[pl.BlockSpec((tm, k), lambda i: (jnp.minimum(i, n1 - 1), 0)),
                  pl.BlockSpec((tm, k), lambda i: (jnp.maximum(i - n1, 0), 0)),
                  pl.BlockSpec((k, n), lambda i: (0, 0)),
                  pl.BlockSpec((tm, n), lambda i: (i, 0))],
        out_specs=pl.BlockSpec((tm, n), lambda i: (i, 0)),
        compiler_params=_params("parallel"),
        name="matmul_residual",
    )(a1, a2, w, res)


def _final_norm_kernel(x_ref, g_ref, o_ref):
    o_ref[...] = _rms(x_ref[...], g_ref[...])


def final_norm(x, g, tm=512):
    t, d = x.shape
    tm = min(tm, t)
    return pl.pallas_call(
        _final_norm_kernel,
        out_shape=jax.ShapeDtypeStruct((t, d), F32),
        grid=(t // tm,),
        in_specs=[pl.BlockSpec((tm, d), lambda i: (i, 0)), pl.BlockSpec((1, d), lambda i: (0, 0))],
        out_specs=pl.BlockSpec((tm, d), lambda i: (i, 0)),
        compiler_params=_params("parallel"),
        name="final_norm",
    )(x, g)


def _sb_prompt_kernel(bias_ref, q_ref, k_ref, v_ref, o_ref, acc_scr, cs_scr):
    hp = pl.program_id(1)
    qi = pl.program_id(2)
    tq = q_ref.shape[0]
    row = lax.broadcasted_iota(jnp.int32, (2 * tq, K_TILE), 0)
    col = lax.broadcasted_iota(jnp.int32, (2 * tq, K_TILE), 1)
    head1 = row >= tq
    q_pos = jnp.where(head1, row - tq, row) + qi * tq
    tri_r = lax.broadcasted_iota(jnp.int32, (2 * K_TILE, K_TILE), 0) & (K_TILE - 1)
    tri_c = lax.broadcasted_iota(jnp.int32, (2 * K_TILE, K_TILE), 1)
    suffix2 = jnp.where(tri_r >= tri_c, 1.0, 0.0).astype(BF16)
    first = lax.broadcasted_iota(jnp.int32, (tq, 2 * HEAD_DIM), 1) < HEAD_DIM
    q = q_ref[...] * (1.0 / math.sqrt(HEAD_DIM))
    q2 = jnp.concatenate([jnp.where(first, q, 0.0), jnp.where(first, 0.0, q)], axis=0).astype(BF16)
    bias = jnp.where(head1, bias_ref[hp * 2 + 1], bias_ref[hp * 2])
    n_diag = tq // K_TILE
    n_kb = (qi + 1) * n_diag

    def tile(kb, cs, masked):
        rows = pl.ds(pl.multiple_of(kb * K_TILE, K_TILE), K_TILE)
        k = k_ref[rows, :].astype(BF16)
        v = v_ref[rows, :].astype(BF16)
        z = lax.dot_general(q2, k, NT_DIMS, preferred_element_type=F32) + bias
        ln = -_softplus(z)
        if masked:
            mask = (col + kb * K_TILE) < q_pos
            ln = jnp.where(mask, ln, 0.0)
        hi, lo = _split_bf16(ln)
        part = jnp.dot(jnp.concatenate([hi, lo], axis=1), suffix2, preferred_element_type=F32)
        w = jnp.exp(z + (part + cs))
        if masked:
            w = jnp.where(mask, w, 0.0)
        return jnp.dot(w.astype(BF16), v, preferred_element_type=F32), cs + part[:, 0:1]

    cs = jnp.zeros((2 * tq, 1), F32)
    acc = None
    for t in range(n_diag):
        pv, cs = tile(n_kb - 1 - t, cs, True)
        acc = pv if acc is None else acc + pv
    acc_scr[...] = acc
    cs_scr[...] = cs

    def body(i, _):
        kb = n_kb - n_diag - 1 - OFF_DIAG_UNROLL * i
        cs = cs_scr[...]
        total = None
        for t in range(OFF_DIAG_UNROLL):
            pv, cs = tile(kb - t, cs, False)
            total = pv if total is None else total + pv
        acc_scr[...] += total
        cs_scr[...] = cs
        return 0

    lax.fori_loop(0, (n_kb - n_diag) // OFF_DIAG_UNROLL, body, 0)
    o_ref[...] = jnp.where(first, acc_scr[0:tq, :], acc_scr[tq:2 * tq, :])


def sb_prompt(qkv, bias, n_batch, seq, tq=256):
    d = qkv.shape[1] // 3
    n_pair = d // (2 * HEAD_DIM)
    tq = min(tq, seq)
    assert (tq // K_TILE) % OFF_DIAG_UNROLL == 0
    n_q = seq // tq
    return pl.pallas_call(
        _sb_prompt_kernel,
        out_shape=jax.ShapeDtypeStruct((n_batch * seq, d), F32),
        grid=(n_batch, n_pair, n_q),
        in_specs=[pl.BlockSpec(memory_space=pltpu.SMEM),
                  pl.BlockSpec((tq, 2 * HEAD_DIM), lambda b, p, i: (b * n_q + i, p)),
                  pl.BlockSpec((seq, 2 * HEAD_DIM), lambda b, p, i: (b, n_pair + p)),
                  pl.BlockSpec((seq, 2 * HEAD_DIM), lambda b, p, i: (b, 2 * n_pair + p))],
        out_specs=pl.BlockSpec((tq, 2 * HEAD_DIM), lambda b, p, i: (b * n_q + i, p)),
        scratch_shapes=[pltpu.VMEM((2 * tq, 2 * HEAD_DIM), F32), pltpu.VMEM((2 * tq, 1), F32)],
        compiler_params=_params("parallel", "parallel", "arbitrary"),
        name="sb_prompt",
    )(bias, qkv, qkv, qkv)


def _sb_sample_kernel(pt_ref, q_ref, kn_ref, vn_ref, bias_ref, *rest):
    del pt_ref
    pps = (len(rest) - 6) // 2
    kc_refs, vc_refs = rest[:pps], rest[pps:2 * pps]
    o_ref, qrep_ref, kpad_ref, vpad_ref, acc_ref, cs_ref = rest[2 * pps:]
    s = pl.program_id(1)
    n_q = q_ref.shape[0]
    d = q_ref.shape[1]
    row = lax.broadcasted_iota(jnp.int32, (LANES, PAGE), 0)
    lane = lax.broadcasted_iota(jnp.int32, (LANES, PAGE), 1)
    tri_r = lax.broadcasted_iota(jnp.int32, (2 * PAGE, PAGE), 0) & (PAGE - 1)
    tri_c = lax.broadcasted_iota(jnp.int32, (2 * PAGE, PAGE), 1)
    suffix2 = jnp.where(tri_r >= tri_c, 1.0, 0.0).astype(BF16)

    def process(kblks, vblks, mask, paged):
        zs, parts = [], []
        for kblk in kblks:
            if paged:
                z = jnp.dot(qrep_ref[...], kblk, preferred_element_type=F32)
            else:
                z = lax.dot_general(qrep_ref[...], kblk, NT_DIMS, preferred_element_type=F32)
            z = z + bias_ref[...]
            ln = -_softplus(z)
            if mask is not None:
                ln = jnp.where(mask, ln, 0.0)
            hi, lo = _split_bf16(ln)
            parts.append(jnp.dot(jnp.concatenate([hi, lo], axis=1), suffix2, preferred_element_type=F32))
            zs.append(z)
        carry = cs_ref[...]
        total = None
        for z, part, vblk in zip(zs, parts, vblks):
            w = jnp.exp(z + (part + carry))
            if mask is not None:
                w = jnp.where(mask, w, 0.0)
            if paged:
                pv = lax.dot_general(w.astype(BF16), vblk, NT_DIMS, preferred_element_type=F32)
            else:
                pv = jnp.dot(w.astype(BF16), vblk, preferred_element_type=F32)
            total = pv if total is None else total + pv
            carry = carry + part[:, 0:1]
        acc_ref[...] += total
        cs_ref[...] = carry

    @pl.when(s == 0)
    def _():
        q = q_ref[...] * (1.0 / math.sqrt(HEAD_DIM))
        qrep = jnp.concatenate([q] * N_HEADS, axis=0)
        r_head = lax.broadcasted_iota(jnp.int32, (N_HEADS * n_q, d), 0) >> (n_q.bit_length() - 1)
        c_head = lax.broadcasted_iota(jnp.int32, (N_HEADS * n_q, d), 1) >> (HEAD_DIM.bit_length() - 1)
        qrep_ref[...] = jnp.where(r_head == c_head, qrep, 0.0).astype(BF16)
        acc_ref[...] = jnp.zeros_like(acc_ref)
        cs_ref[...] = jnp.zeros_like(cs_ref)
        kpad_ref[...] = jnp.zeros_like(kpad_ref)
        vpad_ref[...] = jnp.zeros_like(vpad_ref)
        kpad_ref[0:n_q, :] = kn_ref[...]
        vpad_ref[0:n_q, :] = vn_ref[...]
        mask = lane < (row & (n_q - 1))
        process([kpad_ref[...].astype(BF16)], [vpad_ref[...].astype(BF16)], mask, paged=False)

    @pl.when(s > 0)
    def _():
        process([r[...].reshape(d, PAGE).astype(BF16) for r in kc_refs],
                [r[...].reshape(d, PAGE).astype(BF16) for r in vc_refs], None, paged=True)

    @pl.when(s == pl.num_programs(1) - 1)
    def _():
        half = lax.broadcasted_iota(jnp.int32, (n_q, LANES), 1) < HEAD_DIM
        outs = []
        for j in range(d // LANES):
            a = acc_ref[2 * j * n_q:(2 * j + 1) * n_q, j * LANES:(j + 1) * LANES]
            b = acc_ref[(2 * j + 1) * n_q:(2 * j + 2) * n_q, j * LANES:(j + 1) * LANES]
            outs.append(jnp.where(half, a, b))
        o_ref[...] = jnp.concatenate(outs, axis=-1)


def sb_sample(qkv, row0, n_batch, n_q, page_ids, bias_rows, kcache, vcache, pps=8):
    d = qkv.shape[1] // 3
    n_pages = page_ids.shape[0] // n_batch
    pps = math.gcd(pps, n_pages)
    assert N_HEADS * n_q == LANES and n_q == SUBLANES and row0 % n_q == 0
    rb = row0 // n_q

    def page_spec(r):
        def page_map(b, s, pt):
            return (pt[b * n_pages + n_pages - pps * jnp.maximum(s - 1, 0) - 1 - r], 0, 0, 0)
        return pl.BlockSpec((None, N_HEADS, HEAD_DIM, PAGE), page_map)

    grid_spec = pltpu.PrefetchScalarGridSpec(
        num_scalar_prefetch=1,
        grid=(n_batch, n_pages // pps + 1),
        in_specs=[pl.BlockSpec((n_q, d), lambda b, s, pt: (rb + b, 0)),
                  pl.BlockSpec((n_q, d), lambda b, s, pt: (rb + b, 1)),
                  pl.BlockSpec((n_q, d), lambda b, s, pt: (rb + b, 2)),
                  pl.BlockSpec((LANES, PAGE), lambda b, s, pt: (0, 0))]
                 + [page_spec(r) for r in range(pps)] + [page_spec(r) for r in range(pps)],
        out_specs=pl.BlockSpec((n_q, d), lambda b, s, pt: (b, 0)),
        scratch_shapes=[pltpu.VMEM((LANES, d), BF16),
                        pltpu.VMEM((PAGE, d), F32),
                        pltpu.VMEM((PAGE, d), F32),
                        pltpu.VMEM((LANES, d), F32),
                        pltpu.VMEM((LANES, 1), F32)],
    )
    return pl.pallas_call(
        _sb_sample_kernel,
        out_shape=jax.ShapeDtypeStruct((n_batch * n_q, d), F32),
        grid_spec=grid_spec,
        compiler_params=_params("parallel", "arbitrary"),
        name="sb_sample",
    )(page_ids, qkv, qkv, qkv, bias_rows, *([kcache] * pps), *([vcache] * pps))


def _shifted(u, k, buf_ref, row):
    nb = buf_ref.shape[0]
    s = pltpu.roll(u, k, axis=0)
    for r in range(k):
        s = jnp.where(row == r, buf_ref[nb - k + r:nb - k + r + 1, :], s)
    return s


def _short_conv_kernel(bg_ref, cg_ref, xv_ref, w_ref, buf_ref, g_ref, nb_ref):
    u = cg_ref[...] * xv_ref[...]
    t = u.shape[0]
    row = lax.broadcasted_iota(jnp.int32, u.shape, 0)
    y = _shifted(u, 2, buf_ref, row) * w_ref[0:1, :]
    y = y + _shifted(u, 1, buf_ref, row) * w_ref[1:2, :]
    y = y + u * w_ref[2:3, :]
    g_ref[...] = bg_ref[...] * y
    nb_ref[...] = u[t - 2:t, :]


def short_conv(proj, row0, n_batch, seq, conv_w, buf, tc=512):
    c = proj.shape[1] // 3
    nc = c // tc
    rb = row0 // seq
    return pl.pallas_call(
        _short_conv_kernel,
        out_shape=(jax.ShapeDtypeStruct((n_batch * seq, c), F32),
                   jax.ShapeDtypeStruct((n_batch, 2, c), F32)),
        grid=(n_batch, nc),
        in_specs=[pl.BlockSpec((seq, tc), lambda b, j: (rb + b, j)),
                  pl.BlockSpec((seq, tc), lambda b, j: (rb + b, nc + j)),
                  pl.BlockSpec((seq, tc), lambda b, j: (rb + b, 2 * nc + j)),
                  pl.BlockSpec((3, tc), lambda b, j: (0, j)),
                  pl.BlockSpec((None, 2, tc), lambda b, j: (b, 0, j))],
        out_specs=(pl.BlockSpec((seq, tc), lambda b, j: (b, j)),
                   pl.BlockSpec((None, 2, tc), lambda b, j: (b, 0, j))),
        compiler_params=_params("parallel", "parallel"),
        name="short_conv",
    )(proj, proj, proj, conv_w, buf)


def _scan_tile(a, d, row):
    for k in (1, 2, 4):
        a_s = pltpu.roll(a, k, axis=0)
        d_s = pltpu.roll(d, k, axis=0)
        valid = row >= k
        d = jnp.where(valid, a * d_s + d, d)
        a = jnp.where(valid, a * a_s, a)
    return a, d


def _rglru_kernel(gate_ref, u_ref, cw_ref, cb_ref, wra_ref, bra_ref, wri_ref, bri_ref, lam_ref,
                  buf_ref, h0_ref, o_ref, nb_ref, hl_ref, a_scr, d_scr):
    up = u_ref[...]
    t, c = up.shape
    row = lax.broadcasted_iota(jnp.int32, up.shape, 0)
    y = _shifted(up, 3, buf_ref, row) * cw_ref[0:1, :]
    y = y + _shifted(up, 2, buf_ref, row) * cw_ref[1:2, :]
    y = y + _shifted(up, 1, buf_ref, row) * cw_ref[2:3, :]
    y = y + up * cw_ref[3:4, :]
    u = y + cb_ref[...]
    rs, igs = [], []
    for n in range(c // RG_BLOCK):
        ub = u[:, n * RG_BLOCK:(n + 1) * RG_BLOCK].astype(BF16)
        rs.append(jnp.dot(ub, wra_ref[n], preferred_element_type=F32))
        igs.append(jnp.dot(ub, wri_ref[n], preferred_element_type=F32))
    r = _sigmoid(jnp.concatenate(rs, axis=-1) + bra_ref[...])
    ig = _sigmoid(jnp.concatenate(igs, axis=-1) + bri_ref[...])
    log_a = -RG_C * r * _softplus(-lam_ref[...])
    a = jnp.exp(log_a)
    drive = jnp.sqrt(-jnp.tanh(log_a) * (a * a + 1.0)) * (ig * u)
    a_scr[...] = a
    d_scr[...] = drive
    row8 = lax.broadcasted_iota(jnp.int32, (SUBLANES, c), 0)

    def body(i, h):
        rows = pl.ds(pl.multiple_of(i * SUBLANES, SUBLANES), SUBLANES)
        a_c, d_c = _scan_tile(a_scr[rows, :], d_scr[rows, :], row8)
        hs = a_c * h + d_c
        d_scr[rows, :] = hs
        return hs[SUBLANES - 1:SUBLANES, :]

    h_last = lax.fori_loop(0, t // SUBLANES, body, h0_ref[...])
    o_ref[...] = _gelu(gate_ref[...]) * d_scr[...]
    nb_ref[...] = up[t - 3:t, :]
    hl_ref[...] = h_last


def rglru(proj, row0, n_batch, seq, tc, conv_w, conv_b, w_ra, b_ra, w_ri, b_ri, lam, buf, h0):
    c = proj.shape[1] // 2
    nc = c // tc
    nblk = tc // RG_BLOCK
    rb = row0 // seq
    vec = pl.BlockSpec((1, tc), lambda b, j: (0, j))
    return pl.pallas_call(
        _rglru_kernel,
        out_shape=(jax.ShapeDtypeStruct((n_batch * seq, c), F32),
                   jax.ShapeDtypeStruct((n_batch, 3, c), F32),
                   jax.ShapeDtypeStruct((n_batch, 1, c), F32)),
        grid=(n_batch, nc),
        in_specs=[pl.BlockSpec((seq, tc), lambda b, j: (rb + b, j)),
                  pl.BlockSpec((seq, tc), lambda b, j: (rb + b, nc + j)),
                  pl.BlockSpec((4, tc), lambda b, j: (0, j)),
                  vec,
                  pl.BlockSpec((nblk, RG_BLOCK, RG_BLOCK), lambda b, j: (j, 0, 0)),
                  vec,
                  pl.BlockSpec((nblk, RG_BLOCK, RG_BLOCK), lambda b, j: (j, 0, 0)),
                  vec,
                  vec,
                  pl.BlockSpec((None, 3, tc), lambda b, j: (b, 0, j)),
                  pl.BlockSpec((None, 1, tc), lambda b, j: (b, 0, j))],
        out_specs=(pl.BlockSpec((seq, tc), lambda b, j: (b, j)),
                   pl.BlockSpec((None, 3, tc), lambda b, j: (b, 0, j)),
                   pl.BlockSpec((None, 1, tc), lambda b, j: (b, 0, j))),
        scratch_shapes=[pltpu.VMEM((seq, tc), F32), pltpu.VMEM((seq, tc), F32)],
        compiler_params=_params("parallel", "parallel"),
        name="rglru",
    )(proj, proj, conv_w, conv_b, w_ra, b_ra, w_ri, b_ri, lam, buf, h0)


def _top16(*scores):
    lanes = scores[0].shape[1]
    io_k = lax.broadcasted_iota(jnp.int32, (TOPK, lanes), 0)

    def body(i, carry):
        new = []
        for s, rank, vals in carry:
            n = s.shape[0]
            io = lax.broadcasted_iota(jnp.int32, s.shape, 0)
            m = jnp.max(s, axis=0, keepdims=True)
            idx = jnp.min(jnp.where(s == m, io, n), axis=0, keepdims=True)
            sel = io == idx
            new.append((jnp.where(sel, -jnp.inf, s), jnp.where(sel, i, rank), jnp.where(io_k == i, m, vals)))
        return tuple(new)

    init = tuple((s, jnp.full(s.shape, TOPK, jnp.int32), jnp.zeros((TOPK, lanes), F32)) for s in scores)
    return [(rank, vals) for _, rank, vals in lax.fori_loop(0, TOPK, body, init)]


def _top16_distinct(*scores):
    lanes = scores[0].shape[1]
    io_k = lax.broadcasted_iota(jnp.int32, (TOPK, lanes), 0)

    def body(i, carry):
        new = []
        for s, vals in carry:
            m = jnp.max(s, axis=0, keepdims=True)
            new.append((jnp.where(s == m, -jnp.inf, s), jnp.where(io_k == i, m, vals)))
        return tuple(new)

    init = tuple((s, jnp.zeros((TOPK, lanes), F32)) for s in scores)
    return list(lax.fori_loop(0, TOPK, body, init))


def _removed_count(s_after):
    return jnp.sum(jnp.where(s_after == -jnp.inf, 1.0, 0.0), axis=0, keepdims=True)


def _staircase(v1, v2):
    io8 = lax.broadcasted_iota(jnp.int32, (SUBLANES, v1.shape[1]), 0)
    pad = float(jnp.finfo(F32).min)
    parts = [v1[0:1, :] + v2, v1[1:2, :] + v2[0:SUBLANES, :]]
    for i in range(2, SUBLANES):
        parts.append(jnp.where(io8 < TOPK // (i + 1), v1[i:i + 1, :] + v2[0:SUBLANES, :], pad))
    parts.append(v1[SUBLANES:TOPK, :] + v2[0:1, :])
    return jnp.concatenate(parts, axis=0)


def _staircase_counts(sel):
    rows = [jnp.sum(sel[0:TOPK, :], axis=0, keepdims=True)]
    for i in range(1, SUBLANES):
        lo = TOPK + SUBLANES * (i - 1)
        rows.append(jnp.sum(sel[lo:lo + SUBLANES, :], axis=0, keepdims=True))
    lo = TOPK + SUBLANES * (SUBLANES - 1)
    rows += [sel[lo + r:lo + r + 1, :] for r in range(SUBLANES)]
    return rows


def _peer_route_kernel(x_ref, g_ref, wq_ref, k1_ref, k2_ref, hnt_ref, r2_ref, cnt1_ref, a1_ref, w2_ref, qt_scr):
    h = pl.program_id(1)
    tn = x_ref.shape[0]

    @pl.when(h == 0)
    def _():
        hn_t = _rms(x_ref[...], g_ref[...]).T.astype(BF16)
        hnt_ref[...] = hn_t
        qt_scr[...] = jnp.dot(wq_ref[...], hn_t, preferred_element_type=F32)

    off = pl.multiple_of(h * 2 * N_KEYS, 2 * N_KEYS)
    s1_all = _dot_3pass(k1_ref[...], qt_scr[pl.ds(off, N_KEYS), :])
    s2_all = _dot_3pass(k2_ref[...], qt_scr[pl.ds(off + N_KEYS, N_KEYS), :])
    groups = [slice(lg * LANES, (lg + 1) * LANES) for lg in range(tn // LANES)]

    def route(exact):
        halves = []
        for sl in groups:
            s1, s2 = s1_all[:, sl], s2_all[:, sl]
            if exact:
                (rank1, v1), (rank2, v2) = _top16(s1, s2)
                halves.append((s1, s2, v1, v2, rank1, rank2.astype(F32), None))
            else:
                (left1, v1), (left2, v2) = _top16_distinct(s1, s2)
                r2 = jnp.zeros((N_KEYS, LANES), F32)
                for i in range(TOPK):
                    r2 = r2 + jnp.where(v2[i:i + 1, :] > s2, 1.0, 0.0)
                r2 = jnp.where(left2 == -jnp.inf, r2, float(TOPK))
                off16 = jnp.abs(_removed_count(left1) - TOPK) + jnp.abs(_removed_count(left2) - TOPK)
                halves.append((s1, s2, v1, v2, None, r2, off16))
        cands = [_staircase(hv[2], hv[3]) for hv in halves]
        if exact:
            sels = [rank_c < TOPK for rank_c, _ in _top16(*cands)]
        else:
            lefts = [left_c for left_c, _ in _top16_distinct(*cands)]
            sels = [left_c == -jnp.inf for left_c in lefts]
        ties = None
        for g, sl in enumerate(groups):
            s1, s2, v1, v2, rank1, r2, off16 = halves[g]
            cand, sel = cands[g], sels[g]
            cnt = _staircase_counts(jnp.where(sel, 1.0, 0.0))
            z = jnp.sum(jnp.where(sel, jnp.exp(cand - cand[0:1, :]), 0.0), axis=0, keepdims=True)
            cnt1 = jnp.zeros((N_KEYS, LANES), F32)
            for i in range(TOPK):
                cnt1 = jnp.where((rank1 == i) if exact else (s1 == v1[i:i + 1, :]), cnt[i], cnt1)
            r2_ref[:, sl] = r2.astype(BF16)
            cnt1_ref[:, sl] = cnt1
            a1_ref[:, sl] = jnp.exp(s1 - v1[0:1, :])
            w2_ref[:, sl] = (jnp.exp(s2 - v2[0:1, :]) / z).astype(BF16)
            if not exact:
                off16 = off16 + jnp.abs(_removed_count(lefts[g]) - TOPK)
                ties = off16 if ties is None else ties + off16
        return ties

    ties = route(exact=False)

    @pl.when(jnp.max(ties) > 0.0)
    def _():
        route(exact=True)


def peer_route(x, g, wq_t, keys1, keys2, tn=512):
    t, d = x.shape
    heads = keys1.shape[0]
    tn = min(tn, t)
    side = jax.ShapeDtypeStruct((heads, N_KEYS, t), F32)
    side_bf = jax.ShapeDtypeStruct((heads, N_KEYS, t), BF16)
    side_spec = pl.BlockSpec((None, N_KEYS, tn), lambda i, h: (h, 0, i))
    return pl.pallas_call(
        _peer_route_kernel,
        out_shape=(jax.ShapeDtypeStruct((d, t), BF16), side_bf, side, side, side_bf),
        grid=(t // tn, heads),
        in_specs=[pl.BlockSpec((tn, d), lambda i, h: (i, 0)),
                  pl.BlockSpec((1, d), lambda i, h: (0, 0)),
                  pl.BlockSpec(wq_t.shape, lambda i, h: (0, 0)),
                  pl.BlockSpec((None, N_KEYS, N_KEYS), lambda i, h: (h, 0, 0)),
                  pl.BlockSpec((None, N_KEYS, N_KEYS), lambda i, h: (h, 0, 0))],
        out_specs=(pl.BlockSpec((d, tn), lambda i, h: (0, i)), side_spec, side_spec, side_spec, side_spec),
        scratch_shapes=[pltpu.VMEM((wq_t.shape[0], tn), F32)],
        compiler_params=_params("parallel", "arbitrary"),
        name="peer_route",
    )(x, g, wq_t, keys1, keys2)


def _peer_expert_kernel(x_ref, hnt_ref, u_ref, vt_prev_ref, vt_ref, r2_ref, cnt1_ref, a1_ref, w2_ref, o_ref,
                        acc_scr, pa_scr, pb_scr):
    j = pl.program_id(1)
    n_pair = pl.num_programs(1) - 1
    heads = r2_ref.shape[0]
    eb = u_ref.shape[0] // 2

    def build(p_scr, base):
        hn_t = hnt_ref[...]
        for c0 in range(0, eb, EXPERT_CHUNK):
            ht = jnp.dot(u_ref[base + c0:base + c0 + EXPERT_CHUNK, :], hn_t, preferred_element_type=F32)
            act = _gelu(ht).astype(BF16)
            gates = []
            for e in range((base + c0) // N_KEYS, (base + c0 + EXPERT_CHUNK) // N_KEYS):
                gate = None
                for h in range(heads):
                    c = cnt1_ref[h, e:e + 1, :].astype(BF16)
                    a = a1_ref[h, e:e + 1, :].astype(BF16)
                    wa = w2_ref[h] * a
                    term = jnp.where(r2_ref[h] < c, wa, jnp.zeros_like(wa))
                    gate = term if gate is None else gate + term
                gates.append(gate)
            p_scr[c0:c0 + EXPERT_CHUNK, :] = jnp.concatenate(gates, axis=0) * act

    @pl.when(j == 0)
    def _():
        acc_scr[...] = jnp.zeros_like(acc_scr)
        pb_scr[...] = jnp.zeros_like(pb_scr)

    @pl.when(j < n_pair)
    def _():
        acc_scr[...] += jnp.dot(vt_prev_ref[...], pb_scr[...], preferred_element_type=F32)
        build(pa_scr, 0)
        acc_scr[...] += jnp.dot(vt_ref[...], pa_scr[...], preferred_element_type=F32)
        build(pb_scr, eb)

    @pl.when(j == n_pair)
    def _():
        acc = acc_scr[...] + jnp.dot(vt_prev_ref[...], pb_scr[...], preferred_element_type=F32)
        o_ref[...] = x_ref[...] + acc.T


def peer_experts(x, hn_t, u, vt, r2, cnt1, a1, w2, tn=512, eb=1024):
    t, d = x.shape
    ne = u.shape[0]
    heads = r2.shape[0]
    tn = min(tn, t)
    n_pair = ne // (2 * eb)
    n_grp = 2 * eb // N_KEYS
    last = n_pair - 1
    full = pl.BlockSpec((heads, N_KEYS, tn), lambda i, j: (0, 0, i))
    part = pl.BlockSpec((heads, n_grp, tn), lambda i, j: (0, jnp.minimum(j, last), i))
    return pl.pallas_call(
        _peer_expert_kernel,
        out_shape=jax.ShapeDtypeStruct((t, d), F32),
        grid=(t // tn, n_pair + 1),
        in_specs=[pl.BlockSpec((tn, d), lambda i, j: (i, 0)),
                  pl.BlockSpec((d, tn), lambda i, j: (0, i)),
                  pl.BlockSpec((2 * eb, d), lambda i, j: (jnp.minimum(j, last), 0)),
                  pl.BlockSpec((d, eb), lambda i, j: (0, jnp.maximum(2 * j - 1, 0))),
                  pl.BlockSpec((d, eb), lambda i, j: (0, jnp.minimum(2 * j, 2 * last))),
                  full, part, part, full],
        out_specs=pl.BlockSpec((tn, d), lambda i, j: (i, 0)),
        scratch_shapes=[pltpu.VMEM((d, tn), F32), pltpu.VMEM((eb, tn), BF16), pltpu.VMEM((eb, tn), BF16)],
        compiler_params=_params("parallel", "arbitrary"),
        name="peer_experts",
    )(x, hn_t, u, vt, vt, r2, cnt1, a1, w2)


def peer_ffn_residual(x, g, w_query, keys1, keys2, u_table, v_table):
    wq_t = w_query.T.astype(BF16)
    hn_t, r2, cnt1, a1, w2 = peer_route(x, g, wq_t, keys1, keys2)
    return peer_experts(x, hn_t, u_table.astype(BF16), v_table.T.astype(BF16), r2, cnt1, a1, w2)


def kernel(x_prompt, x_sample, cache_k_sb, cache_v_sb, page_table, state_conv_sc, state_conv_rg, state_h_rg, norm_mix, norm_ffn, norm_final, w_qkv_sb, w_o_sb, b_sb, w_in_sc, conv_w_sc, w_out_sc, w_in_rg, conv_w_rg, conv_b_rg, w_ra_rg, b_ra_rg, w_ri_rg, b_ri_rg, lam_rg, w_out_rg, w_query_peer, keys1_peer, keys2_peer, u_peer, v_peer):
    bp, seq, d = x_prompt.shape
    bs, dseq, _ = x_sample.shape
    n_p = bp * seq
    depth = norm_mix.shape[0]
    n_pool = cache_k_sb.shape[1]
    n_pages = page_table.shape[1]
    x = jnp.concatenate([x_prompt.reshape(n_p, d), x_sample.reshape(bs * dseq, d)], axis=0)
    kcache = jnp.transpose(cache_k_sb, (0, 1, 3, 4, 2)).reshape(-1, N_HEADS, HEAD_DIM, PAGE)
    vcache = jnp.transpose(cache_v_sb, (0, 1, 3, 4, 2)).reshape(-1, N_HEADS, HEAD_DIM, PAGE)
    k_p, v_p, k_s, v_s = [], [], [], []
    csc_p, csc_s, crg_p, crg_s, h_p, h_s = [], [], [], [], [], []
    for layer in range(depth):
        kind, j = layer % 3, layer // 3
        g_mix = norm_mix[layer][None, :]
        if kind == 0:
            qkv = norm_matmul(x, g_mix, w_qkv_sb[j].astype(BF16))
            o_p = sb_prompt(qkv, b_sb[j], bp, seq)
            page_ids = (page_table + j * n_pool).reshape(-1)
            bias_rows = jnp.broadcast_to(jnp.repeat(b_sb[j], dseq)[:, None], (N_HEADS * dseq, PAGE))
            o_s = sb_sample(qkv, n_p, bs, dseq, page_ids, bias_rows, kcache, vcache)
            mix = (o_p, o_s)
            w_out = w_o_sb[j]
            k_p.append(qkv[:n_p, d:2 * d].reshape(bp, seq, N_HEADS, HEAD_DIM))
            v_p.append(qkv[:n_p, 2 * d:].reshape(bp, seq, N_HEADS, HEAD_DIM))
            k_s.append(qkv[n_p:, d:2 * d].reshape(bs, dseq, N_HEADS, HEAD_DIM))
            v_s.append(qkv[n_p:, 2 * d:].reshape(bs, dseq, N_HEADS, HEAD_DIM))
        elif kind == 1:
            proj = norm_matmul(x, g_mix, w_in_sc[j].astype(BF16))
            c = conv_w_sc.shape[2]
            g_p, nb_p = short_conv(proj, 0, bp, seq, conv_w_sc[j], jnp.zeros((bp, 2, c), F32))
            g_s, nb_s = short_conv(proj, n_p, bs, dseq, conv_w_sc[j], state_conv_sc[j])
            mix = (g_p, g_s)
            w_out = w_out_sc[j]
            csc_p.append(nb_p)
            csc_s.append(nb_s)
        else:
            proj = norm_matmul(x, g_mix, w_in_rg[j].astype(BF16))
            c = conv_w_rg.shape[2]
            wts = (conv_w_rg[j], conv_b_rg[j][None, :], w_ra_rg[j].astype(BF16), b_ra_rg[j][None, :],
                   w_ri_rg[j].astype(BF16), b_ri_rg[j][None, :], lam_rg[j][None, :])
            g_p, nb_p, hl_p = rglru(proj, 0, bp, seq, 2 * RG_BLOCK, *wts,
                                    jnp.zeros((bp, 3, c), F32), jnp.zeros((bp, 1, c), F32))
            g_s, nb_s, hl_s = rglru(proj, n_p, bs, dseq, c, *wts, state_conv_rg[j], state_h_rg[j][:, None, :])
            mix = (g_p, g_s)
            w_out = w_out_rg[j]
            crg_p.append(nb_p)
            crg_s.append(nb_s)
            h_p.append(hl_p[:, 0, :])
            h_s.append(hl_s[:, 0, :])
        x = matmul_residual(*mix, w_out.astype(BF16), x)
        x = peer_ffn_residual(x, norm_ffn[layer][None, :], w_query_peer[layer], keys1_peer[layer],
                              keys2_peer[layer], u_peer[layer], v_peer[layer])
    y = final_norm(x, norm_final[None, :])
    return (y[:n_p].reshape(bp, seq, d), y[n_p:].reshape(bs, dseq, d),
            jnp.stack(k_p), jnp.stack(v_p), jnp.stack(csc_p), jnp.stack(crg_p), jnp.stack(h_p),
            jnp.stack(k_s), jnp.stack(v_s), jnp.stack(csc_s), jnp.stack(crg_s), jnp.stack(h_s))
```

```python
import functools
import math

import jax
import jax.numpy as jnp
from jax import lax
from jax.experimental import pallas as pl
from jax.experimental.pallas import tpu as pltpu

F32 = jnp.float32
BF16 = jnp.bfloat16
EPS = 1e-6
HEAD_DIM = 64
N_HEADS = 16
PAGE = 128
K_TILE = 128
OFF_DIAG_UNROLL = 4
RG_BLOCK = 128
RG_C = 8.0
TOPK = 16
N_KEYS = 128
EXPERT_CHUNK = 256
LANES = 128
SUBLANES = 8
VMEM_LIMIT = 48 * 1024 * 1024

NT_DIMS = (((1,), (1,)), ((), ()))
TN_DIMS = (((0,), (0,)), ((), ()))


def _params(*sem):
    return pltpu.CompilerParams(dimension_semantics=sem, vmem_limit_bytes=VMEM_LIMIT)


def _softplus(z):
    return jnp.maximum(z, 0.0) + jnp.log(1.0 + jnp.exp(-jnp.abs(z)))


def _sigmoid(z):
    return 1.0 / (1.0 + jnp.exp(-z))


def _gelu(x):
    return 0.5 * x * (1.0 + lax.erf(x * (1.0 / math.sqrt(2.0))))


def _rms(x, g):
    ms = jnp.mean(x * x, axis=-1, keepdims=True)
    return x * lax.rsqrt(ms + EPS) * g


def _split_bf16(x):
    hi = x.astype(BF16)
    lo = (x - hi.astype(F32)).astype(BF16)
    return hi, lo


def _dot_3pass(a, b):
    a_hi, a_lo = _split_bf16(a)
    b_hi, b_lo = _split_bf16(b)
    return (jnp.dot(a_hi, b_hi, preferred_element_type=F32)
            + (jnp.dot(a_hi, b_lo, preferred_element_type=F32) + jnp.dot(a_lo, b_hi, preferred_element_type=F32)))


def _norm_mm_kernel(x_ref, g_ref, w_ref, o_ref, hn_ref):
    @pl.when(pl.program_id(1) == 0)
    def _():
        hn_ref[...] = _rms(x_ref[...], g_ref[...]).astype(BF16)

    o_ref[...] = jnp.dot(hn_ref[...], w_ref[...], preferred_element_type=F32)


def norm_matmul(x, g, w, tm=512, tn=1024):
    t, d = x.shape
    n = w.shape[1]
    tm = min(tm, t)
    tn = min(tn, n)
    return pl.pallas_call(
        _norm_mm_kernel,
        out_shape=jax.ShapeDtypeStruct((t, n), F32),
        grid=(t // tm, n // tn),
        in_specs=[pl.BlockSpec((tm, d), lambda i, j: (i, 0)),
                  pl.BlockSpec((1, d), lambda i, j: (0, 0)),
                  pl.BlockSpec((d, tn), lambda i, j: (0, j))],
        out_specs=pl.BlockSpec((tm, tn), lambda i, j: (i, j)),
        scratch_shapes=[pltpu.VMEM((tm, d), BF16)],
        compiler_params=_params("parallel", "arbitrary"),
        name="norm_matmul",
    )(x, g, w)


def _mm_res_kernel(n_first, a1_ref, a2_ref, w_ref, r_ref, o_ref):
    i = pl.program_id(0)

    @pl.when(i < n_first)
    def _():
        o_ref[...] = r_ref[...] + jnp.dot(a1_ref[...].astype(BF16), w_ref[...], preferred_element_type=F32)

    @pl.when(i >= n_first)
    def _():
        o_ref[...] = r_ref[...] + jnp.dot(a2_ref[...].astype(BF16), w_ref[...], preferred_element_type=F32)


def matmul_residual(a1, a2, w, res, tm=512):
    t1, k = a1.shape
    t2 = a2.shape[0]
    n = w.shape[1]
    tm = math.gcd(tm, math.gcd(t1, t2))
    n1, n2 = t1 // tm, t2 // tm
    return pl.pallas_call(
        functools.partial(_mm_res_kernel, n1),
        out_shape=jax.ShapeDtypeStruct((t1 + t2, n), F32),
        grid=(n1 + n2,),
        in_specs=[pl.BlockSpec((tm, k), lambda i: (jnp.minimum(i, n1 - 1), 0)),
                  pl.BlockSpec((tm, k), lambda i: (jnp.maximum(i - n1, 0), 0)),
                  pl.BlockSpec((k, n), lambda i: (0, 0)),
                  pl.BlockSpec((tm, n), lambda i: (i, 0))],
        out_specs=pl.BlockSpec((tm, n), lambda i: (i, 0)),
        compiler_params=_params("parallel"),
        name="matmul_residual",
    )(a1, a2, w, res)


def _final_norm_kernel(x_ref, g_ref, o_ref):
    o_ref[...] = _rms(x_ref[...], g_ref[...])


def final_norm(x, g, tm=512):
    t, d = x.shape
    tm = min(tm, t)
    return pl.pallas_call(
        _final_norm_kernel,
        out_shape=jax.ShapeDtypeStruct((t, d), F32),
        grid=(t // tm,),
        in_specs=[pl.BlockSpec((tm, d), lambda i: (i, 0)), pl.BlockSpec((1, d), lambda i: (0, 0))],
        out_specs=pl.BlockSpec((tm, d), lambda i: (i, 0)),
        compiler_params=_params("parallel"),
        name="final_norm",
    )(x, g)


def _sb_prompt_kernel(bias_ref, q_ref, k_ref, v_ref, o_ref, acc_scr, cs_scr):
    hp = pl.program_id(1)
    qi = pl.program_id(2)
    tq = q_ref.shape[0]
    row = lax.broadcasted_iota(jnp.int32, (2 * tq, K_TILE), 0)
    col = lax.broadcasted_iota(jnp.int32, (2 * tq, K_TILE), 1)
    head1 = row >= tq
    q_pos = jnp.where(head1, row - tq, row) + qi * tq
    tri_r = lax.broadcasted_iota(jnp.int32, (2 * K_TILE, K_TILE), 0) & (K_TILE - 1)
    tri_c = lax.broadcasted_iota(jnp.int32, (2 * K_TILE, K_TILE), 1)
    suffix2 = jnp.where(tri_r >= tri_c, 1.0, 0.0).astype(BF16)
    first = lax.broadcasted_iota(jnp.int32, (tq, 2 * HEAD_DIM), 1) < HEAD_DIM
    q = q_ref[...] * (1.0 / math.sqrt(HEAD_DIM))
    q2 = jnp.concatenate([jnp.where(first, q, 0.0), jnp.where(first, 0.0, q)], axis=0).astype(BF16)
    bias = jnp.where(head1, bias_ref[hp * 2 + 1], bias_ref[hp * 2])
    n_diag = tq // K_TILE
    n_kb = (qi + 1) * n_diag

    def tile(kb, cs, masked):
        rows = pl.ds(pl.multiple_of(kb * K_TILE, K_TILE), K_TILE)
        k = k_ref[rows, :].astype(BF16)
        v = v_ref[rows, :].astype(BF16)
        z = lax.dot_general(q2, k, NT_DIMS, preferred_element_type=F32) + bias
        ln = -_softplus(z)
        if masked:
            mask = (col + kb * K_TILE) < q_pos
            ln = jnp.where(mask, ln, 0.0)
        hi, lo = _split_bf16(ln)
        part = jnp.dot(jnp.concatenate([hi, lo], axis=1), suffix2, preferred_element_type=F32)
        w = jnp.exp(z + (part + cs))
        if masked:
            w = jnp.where(mask, w, 0.0)
        return jnp.dot(w.astype(BF16), v, preferred_element_type=F32), cs + part[:, 0:1]

    cs = jnp.zeros((2 * tq, 1), F32)
    acc = None
    for t in range(n_diag):
        pv, cs = tile(n_kb - 1 - t, cs, True)
        acc = pv if acc is None else acc + pv
    acc_scr[...] = acc
    cs_scr[...] = cs

    def off_diagonal(kb, count):
        cs = cs_scr[...]
        total = None
        for t in range(count):
            pv, cs = tile(kb - t, cs, False)
            total = pv if total is None else total + pv
        acc_scr[...] += total
        cs_scr[...] = cs

    n_off = n_kb - n_diag

    def body(i, _):
        off_diagonal(n_off - 1 - OFF_DIAG_UNROLL * i, OFF_DIAG_UNROLL)
        return 0

    lax.fori_loop(0, n_off // OFF_DIAG_UNROLL, body, 0)
    rest = n_diag % OFF_DIAG_UNROLL
    if rest:
        @pl.when(n_off % OFF_DIAG_UNROLL != 0)
        def _():
            off_diagonal(rest - 1, rest)

    o_ref[...] = jnp.where(first, acc_scr[0:tq, :], acc_scr[tq:2 * tq, :])


def sb_prompt(qkv, bias, n_batch, seq, tq=256):
    d = qkv.shape[1] // 3
    n_pair = d // (2 * HEAD_DIM)
    tq = min(tq, seq)
    assert OFF_DIAG_UNROLL == 2 * (tq // K_TILE) or (tq // K_TILE) % OFF_DIAG_UNROLL == 0
    n_q = seq // tq
    return pl.pallas_call(
        _sb_prompt_kernel,
        out_shape=jax.ShapeDtypeStruct((n_batch * seq, d), F32),
        grid=(n_batch, n_pair, n_q),
        in_specs=[pl.BlockSpec(memory_space=pltpu.SMEM),
                  pl.BlockSpec((tq, 2 * HEAD_DIM), lambda b, p, i: (b * n_q + i, p)),
                  pl.BlockSpec((seq, 2 * HEAD_DIM), lambda b, p, i: (b, n_pair + p)),
                  pl.BlockSpec((seq, 2 * HEAD_DIM), lambda b, p, i: (b, 2 * n_pair + p))],
        out_specs=pl.BlockSpec((tq, 2 * HEAD_DIM), lambda b, p, i: (b * n_q + i, p)),
        scratch_shapes=[pltpu.VMEM((2 * tq, 2 * HEAD_DIM), F32), pltpu.VMEM((2 * tq, 1), F32)],
        compiler_params=_params("parallel", "parallel", "arbitrary"),
        name="sb_prompt",
    )(bias, qkv, qkv, qkv)


def _sb_sample_kernel(pt_ref, q_ref, kn_ref, vn_ref, bias_ref, *rest):
    del pt_ref
    pps = (len(rest) - 6) // 2
    kc_refs, vc_refs = rest[:pps], rest[pps:2 * pps]
    o_ref, qrep_ref, kpad_ref, vpad_ref, acc_ref, cs_ref = rest[2 * pps:]
    s = pl.program_id(1)
    n_q = q_ref.shape[0]
    d = q_ref.shape[1]
    row = lax.broadcasted_iota(jnp.int32, (LANES, PAGE), 0)
    lane = lax.broadcasted_iota(jnp.int32, (LANES, PAGE), 1)
    tri_r = lax.broadcasted_iota(jnp.int32, (2 * PAGE, PAGE), 0) & (PAGE - 1)
    tri_c = lax.broadcasted_iota(jnp.int32, (2 * PAGE, PAGE), 1)
    suffix2 = jnp.where(tri_r >= tri_c, 1.0, 0.0).astype(BF16)

    def process(kblks, vblks, mask, paged):
        zs, parts = [], []
        for kblk in kblks:
            if paged:
                z = jnp.dot(qrep_ref[...], kblk, preferred_element_type=F32)
            else:
                z = lax.dot_general(qrep_ref[...], kblk, NT_DIMS, preferred_element_type=F32)
            z = z + bias_ref[...]
            ln = -_softplus(z)
            if mask is not None:
                ln = jnp.where(mask, ln, 0.0)
            hi, lo = _split_bf16(ln)
            parts.append(jnp.dot(jnp.concatenate([hi, lo], axis=1), suffix2, preferred_element_type=F32))
            zs.append(z)
        carry = cs_ref[...]
        total = None
        for z, part, vblk in zip(zs, parts, vblks):
            w = jnp.exp(z + (part + carry))
            if mask is not None:
                w = jnp.where(mask, w, 0.0)
            if paged:
                pv = lax.dot_general(w.astype(BF16), vblk, NT_DIMS, preferred_element_type=F32)
            else:
                pv = jnp.dot(w.astype(BF16), vblk, preferred_element_type=F32)
            total = pv if total is None else total + pv
            carry = carry + part[:, 0:1]
        acc_ref[...] += total
        cs_ref[...] = carry

    @pl.when(s == 0)
    def _():
        q = q_ref[...] * (1.0 / math.sqrt(HEAD_DIM))
        qrep = jnp.concatenate([q] * N_HEADS, axis=0)
        r_head = lax.broadcasted_iota(jnp.int32, (N_HEADS * n_q, d), 0) >> (n_q.bit_length() - 1)
        c_head = lax.broadcasted_iota(jnp.int32, (N_HEADS * n_q, d), 1) >> (HEAD_DIM.bit_length() - 1)
        qrep_ref[...] = jnp.where(r_head == c_head, qrep, 0.0).astype(BF16)
        acc_ref[...] = jnp.zeros_like(acc_ref)
        cs_ref[...] = jnp.zeros_like(cs_ref)
        kpad_ref[...] = jnp.zeros_like(kpad_ref)
        vpad_ref[...] = jnp.zeros_like(vpad_ref)
        kpad_ref[0:n_q, :] = kn_ref[...]
        vpad_ref[0:n_q, :] = vn_ref[...]
        mask = lane < (row & (n_q - 1))
        process([kpad_ref[...].astype(BF16)], [vpad_ref[...].astype(BF16)], mask, paged=False)

    @pl.when(s > 0)
    def _():
        process([r[...].reshape(d, PAGE).astype(BF16) for r in kc_refs],
                [r[...].reshape(d, PAGE).astype(BF16) for r in vc_refs], None, paged=True)

    @pl.when(s == pl.num_programs(1) - 1)
    def _():
        half = lax.broadcasted_iota(jnp.int32, (n_q, LANES), 1) < HEAD_DIM
        outs = []
        for j in range(d // LANES):
            a = acc_ref[2 * j * n_q:(2 * j + 1) * n_q, j * LANES:(j + 1) * LANES]
            b = acc_ref[(2 * j + 1) * n_q:(2 * j + 2) * n_q, j * LANES:(j + 1) * LANES]
            outs.append(jnp.where(half, a, b))
        o_ref[...] = jnp.concatenate(outs, axis=-1)


def sb_sample(qkv, row0, n_batch, n_q, page_ids, bias_rows, kcache, vcache, pps=8):
    d = qkv.shape[1] // 3
    n_pages = page_ids.shape[0] // n_batch
    pps = math.gcd(pps, n_pages)
    assert N_HEADS * n_q == LANES and n_q == SUBLANES and row0 % n_q == 0
    rb = row0 // n_q

    def page_spec(r):
        def page_map(b, s, pt):
            return (pt[b * n_pages + n_pages - pps * jnp.maximum(s - 1, 0) - 1 - r], 0, 0, 0)
        return pl.BlockSpec((None, N_HEADS, HEAD_DIM, PAGE), page_map)

    grid_spec = pltpu.PrefetchScalarGridSpec(
        num_scalar_prefetch=1,
        grid=(n_batch, n_pages // pps + 1),
        in_specs=[pl.BlockSpec((n_q, d), lambda b, s, pt: (rb + b, 0)),
                  pl.BlockSpec((n_q, d), lambda b, s, pt: (rb + b, 1)),
                  pl.BlockSpec((n_q, d), lambda b, s, pt: (rb + b, 2)),
                  pl.BlockSpec((LANES, PAGE), lambda b, s, pt: (0, 0))]
                 + [page_spec(r) for r in range(pps)] + [page_spec(r) for r in range(pps)],
        out_specs=pl.BlockSpec((n_q, d), lambda b, s, pt: (b, 0)),
        scratch_shapes=[pltpu.VMEM((LANES, d), BF16),
                        pltpu.VMEM((PAGE, d), F32),
                        pltpu.VMEM((PAGE, d), F32),
                        pltpu.VMEM((LANES, d), F32),
                        pltpu.VMEM((LANES, 1), F32)],
    )
    return pl.pallas_call(
        _sb_sample_kernel,
        out_shape=jax.ShapeDtypeStruct((n_batch * n_q, d), F32),
        grid_spec=grid_spec,
        compiler_params=_params("parallel", "arbitrary"),
        name="sb_sample",
    )(page_ids, qkv, qkv, qkv, bias_rows, *([kcache] * pps), *([vcache] * pps))


def _shifted(u, k, buf_ref, row):
    nb = buf_ref.shape[0]
    s = pltpu.roll(u, k, axis=0)
    for r in range(k):
        s = jnp.where(row == r, buf_ref[nb - k + r:nb - k + r + 1, :], s)
    return s


def _short_conv_kernel(bg_ref, cg_ref, xv_ref, w_ref, buf_ref, g_ref, nb_ref):
    u = cg_ref[...] * xv_ref[...]
    t = u.shape[0]
    row = lax.broadcasted_iota(jnp.int32, u.shape, 0)
    y = _shifted(u, 2, buf_ref, row) * w_ref[0:1, :]
    y = y + _shifted(u, 1, buf_ref, row) * w_ref[1:2, :]
    y = y + u * w_ref[2:3, :]
    g_ref[...] = bg_ref[...] * y
    nb_ref[...] = u[t - 2:t, :]


def short_conv(proj, row0, n_batch, seq, conv_w, buf, tc=512):
    c = proj.shape[1] // 3
    nc = c // tc
    rb = row0 // seq
    return pl.pallas_call(
        _short_conv_kernel,
        out_shape=(jax.ShapeDtypeStruct((n_batch * seq, c), F32),
                   jax.ShapeDtypeStruct((n_batch, 2, c), F32)),
        grid=(n_batch, nc),
        in_specs=[pl.BlockSpec((seq, tc), lambda b, j: (rb + b, j)),
                  pl.BlockSpec((seq, tc), lambda b, j: (rb + b, nc + j)),
                  pl.BlockSpec((seq, tc), lambda b, j: (rb + b, 2 * nc + j)),
                  pl.BlockSpec((3, tc), lambda b, j: (0, j)),
                  pl.BlockSpec((None, 2, tc), lambda b, j: (b, 0, j))],
        out_specs=(pl.BlockSpec((seq, tc), lambda b, j: (b, j)),
                   pl.BlockSpec((None, 2, tc), lambda b, j: (b, 0, j))),
        compiler_params=_params("parallel", "parallel"),
        name="short_conv",
    )(proj, proj, proj, conv_w, buf)


def _scan_tile(a, d, row):
    for k in (1, 2, 4):
        a_s = pltpu.roll(a, k, axis=0)
        d_s = pltpu.roll(d, k, axis=0)
        valid = row >= k
        d = jnp.where(valid, a * d_s + d, d)
        a = jnp.where(valid, a * a_s, a)
    return a, d


def _rglru_kernel(gate_ref, u_ref, cw_ref, cb_ref, wra_ref, bra_ref, wri_ref, bri_ref, lam_ref,
                  buf_ref, h0_ref, o_ref, nb_ref, hl_ref, a_scr, d_scr):
    up = u_ref[...]
    t, c = up.shape
    row = lax.broadcasted_iota(jnp.int32, up.shape, 0)
    y = _shifted(up, 3, buf_ref, row) * cw_ref[0:1, :]
    y = y + _shifted(up, 2, buf_ref, row) * cw_ref[1:2, :]
    y = y + _shifted(up, 1, buf_ref, row) * cw_ref[2:3, :]
    y = y + up * cw_ref[3:4, :]
    u = y + cb_ref[...]
    rs, igs = [], []
    for n in range(c // RG_BLOCK):
        ub = u[:, n * RG_BLOCK:(n + 1) * RG_BLOCK].astype(BF16)
        rs.append(jnp.dot(ub, wra_ref[n], preferred_element_type=F32))
        igs.append(jnp.dot(ub, wri_ref[n], preferred_element_type=F32))
    r = _sigmoid(jnp.concatenate(rs, axis=-1) + bra_ref[...])
    ig = _sigmoid(jnp.concatenate(igs, axis=-1) + bri_ref[...])
    log_a = -RG_C * r * _softplus(-lam_ref[...])
    a = jnp.exp(log_a)
    drive = jnp.sqrt(-jnp.tanh(log_a) * (a * a + 1.0)) * (ig * u)
    a_scr[...] = a
    d_scr[...] = drive
    row8 = lax.broadcasted_iota(jnp.int32, (SUBLANES, c), 0)

    def body(i, h):
        rows = pl.ds(pl.multiple_of(i * SUBLANES, SUBLANES), SUBLANES)
        a_c, d_c = _scan_tile(a_scr[rows, :], d_scr[rows, :], row8)
        hs = a_c * h + d_c
        d_scr[rows, :] = hs
        return hs[SUBLANES - 1:SUBLANES, :]

    h_last = lax.fori_loop(0, t // SUBLANES, body, h0_ref[...])
    o_ref[...] = _gelu(gate_ref[...]) * d_scr[...]
    nb_ref[...] = up[t - 3:t, :]
    hl_ref[...] = h_last


def rglru(proj, row0, n_batch, seq, tc, conv_w, conv_b, w_ra, b_ra, w_ri, b_ri, lam, buf, h0):
    c = proj.shape[1] // 2
    nc = c // tc
    nblk = tc // RG_BLOCK
    rb = row0 // seq
    vec = pl.BlockSpec((1, tc), lambda b, j: (0, j))
    return pl.pallas_call(
        _rglru_kernel,
        out_shape=(jax.ShapeDtypeStruct((n_batch * seq, c), F32),
                   jax.ShapeDtypeStruct((n_batch, 3, c), F32),
                   jax.ShapeDtypeStruct((n_batch, 1, c), F32)),
        grid=(n_batch, nc),
        in_specs=[pl.BlockSpec((seq, tc), lambda b, j: (rb + b, j)),
                  pl.BlockSpec((seq, tc), lambda b, j: (rb + b, nc + j)),
                  pl.BlockSpec((4, tc), lambda b, j: (0, j)),
                  vec,
                  pl.BlockSpec((nblk, RG_BLOCK, RG_BLOCK), lambda b, j: (j, 0, 0)),
                  vec,
                  pl.BlockSpec((nblk, RG_BLOCK, RG_BLOCK), lambda b, j: (j, 0, 0)),
                  vec,
                  vec,
                  pl.BlockSpec((None, 3, tc), lambda b, j: (b, 0, j)),
                  pl.BlockSpec((None, 1, tc), lambda b, j: (b, 0, j))],
        out_specs=(pl.BlockSpec((seq, tc), lambda b, j: (b, j)),
                   pl.BlockSpec((None, 3, tc), lambda b, j: (b, 0, j)),
                   pl.BlockSpec((None, 1, tc), lambda b, j: (b, 0, j))),
        scratch_shapes=[pltpu.VMEM((seq, tc), F32), pltpu.VMEM((seq, tc), F32)],
        compiler_params=_params("parallel", "parallel"),
        name="rglru",
    )(proj, proj, conv_w, conv_b, w_ra, b_ra, w_ri, b_ri, lam, buf, h0)


def _top16(*scores):
    lanes = scores[0].shape[1]
    io_k = lax.broadcasted_iota(jnp.int32, (TOPK, lanes), 0)

    def body(i, carry):
        new = []
        for s, rank, vals in carry:
            n = s.shape[0]
            io = lax.broadcasted_iota(jnp.int32, s.shape, 0)
            m = jnp.max(s, axis=0, keepdims=True)
            idx = jnp.min(jnp.where(s == m, io, n), axis=0, keepdims=True)
            sel = io == idx
            new.append((jnp.where(sel, -jnp.inf, s), jnp.where(sel, i, rank), jnp.where(io_k == i, m, vals)))
        return tuple(new)

    init = tuple((s, jnp.full(s.shape, TOPK, jnp.int32), jnp.zeros((TOPK, lanes), F32)) for s in scores)
    return [(rank, vals) for _, rank, vals in lax.fori_loop(0, TOPK, body, init)]


def _top16_distinct(*scores):
    lanes = scores[0].shape[1]
    io_k = lax.broadcasted_iota(jnp.int32, (TOPK, lanes), 0)

    def body(i, carry):
        new = []
        for s, vals in carry:
            m = jnp.max(s, axis=0, keepdims=True)
            new.append((jnp.where(s == m, -jnp.inf, s), jnp.where(io_k == i, m, vals)))
        return tuple(new)

    init = tuple((s, jnp.zeros((TOPK, lanes), F32)) for s in scores)
    return list(lax.fori_loop(0, TOPK, body, init))


def _removed_count(s_after):
    return jnp.sum(jnp.where(s_after == -jnp.inf, 1.0, 0.0), axis=0, keepdims=True)


def _staircase(v1, v2):
    io8 = lax.broadcasted_iota(jnp.int32, (SUBLANES, v1.shape[1]), 0)
    pad = float(jnp.finfo(F32).min)
    parts = [v1[0:1, :] + v2, v1[1:2, :] + v2[0:SUBLANES, :]]
    for i in range(2, SUBLANES):
        parts.append(jnp.where(io8 < TOPK // (i + 1), v1[i:i + 1, :] + v2[0:SUBLANES, :], pad))
    parts.append(v1[SUBLANES:TOPK, :] + v2[0:1, :])
    return jnp.concatenate(parts, axis=0)


def _staircase_counts(sel):
    rows = [jnp.sum(sel[0:TOPK, :], axis=0, keepdims=True)]
    for i in range(1, SUBLANES):
        lo = TOPK + SUBLANES * (i - 1)
        rows.append(jnp.sum(sel[lo:lo + SUBLANES, :], axis=0, keepdims=True))
    lo = TOPK + SUBLANES * (SUBLANES - 1)
    rows += [sel[lo + r:lo + r + 1, :] for r in range(SUBLANES)]
    return rows


def _peer_route_kernel(x_ref, g_ref, wq_ref, k1_ref, k2_ref, hnt_ref, r2_ref, cnt1_ref, a1_ref, w2_ref, qt_scr):
    h = pl.program_id(1)
    tn = x_ref.shape[0]

    @pl.when(h == 0)
    def _():
        hn_t = _rms(x_ref[...], g_ref[...]).T.astype(BF16)
        hnt_ref[...] = hn_t
        qt_scr[...] = jnp.dot(wq_ref[...], hn_t, preferred_element_type=F32)

    off = pl.multiple_of(h * 2 * N_KEYS, 2 * N_KEYS)
    s1_all = _dot_3pass(k1_ref[...], qt_scr[pl.ds(off, N_KEYS), :])
    s2_all = _dot_3pass(k2_ref[...], qt_scr[pl.ds(off + N_KEYS, N_KEYS), :])
    groups = [slice(lg * LANES, (lg + 1) * LANES) for lg in range(tn // LANES)]

    def route(exact):
        halves = []
        for sl in groups:
            s1, s2 = s1_all[:, sl], s2_all[:, sl]
            if exact:
                (rank1, v1), (rank2, v2) = _top16(s1, s2)
                halves.append((s1, s2, v1, v2, rank1, rank2.astype(F32), None))
            else:
                (left1, v1), (left2, v2) = _top16_distinct(s1, s2)
                r2 = jnp.zeros((N_KEYS, LANES), F32)
                for i in range(TOPK):
                    r2 = r2 + jnp.where(v2[i:i + 1, :] > s2, 1.0, 0.0)
                r2 = jnp.where(left2 == -jnp.inf, r2, float(TOPK))
                off16 = jnp.abs(_removed_count(left1) - TOPK) + jnp.abs(_removed_count(left2) - TOPK)
                halves.append((s1, s2, v1, v2, None, r2, off16))
        cands = [_staircase(hv[2], hv[3]) for hv in halves]
        if exact:
            sels = [rank_c < TOPK for rank_c, _ in _top16(*cands)]
        else:
            lefts = [left_c for left_c, _ in _top16_distinct(*cands)]
            sels = [left_c == -jnp.inf for left_c in lefts]
        ties = None
        for g, sl in enumerate(groups):
            s1, s2, v1, v2, rank1, r2, off16 = halves[g]
            cand, sel = cands[g], sels[g]
            cnt = _staircase_counts(jnp.where(sel, 1.0, 0.0))
            z = jnp.sum(jnp.where(sel, jnp.exp(cand - cand[0:1, :]), 0.0), axis=0, keepdims=True)
            cnt1 = jnp.zeros((N_KEYS, LANES), F32)
            for i in range(TOPK):
                cnt1 = jnp.where((rank1 == i) if exact else (s1 == v1[i:i + 1, :]), cnt[i], cnt1)
            r2_ref[:, sl] = r2.astype(BF16)
            cnt1_ref[:, sl] = cnt1
            a1_ref[:, sl] = jnp.exp(s1 - v1[0:1, :])
            w2_ref[:, sl] = (jnp.exp(s2 - v2[0:1, :]) / z).astype(BF16)
            if not exact:
                off16 = off16 + jnp.abs(_removed_count(lefts[g]) - TOPK)
                ties = off16 if ties is None else ties + off16
        return ties

    ties = route(exact=False)

    @pl.when(jnp.max(ties) > 0.0)
    def _():
        route(exact=True)


def peer_route(x, g, wq_t, keys1, keys2, tn=512):
    t, d = x.shape
    heads = keys1.shape[0]
    tn = min(tn, t)
    side = jax.ShapeDtypeStruct((heads, N_KEYS, t), F32)
    side_bf = jax.ShapeDtypeStruct((heads, N_KEYS, t), BF16)
    side_spec = pl.BlockSpec((None, N_KEYS, tn), lambda i, h: (h, 0, i))
    return pl.pallas_call(
        _peer_route_kernel,
        out_shape=(jax.ShapeDtypeStruct((d, t), BF16), side_bf, side, side, side_bf),
        grid=(t // tn, heads),
        in_specs=[pl.BlockSpec((tn, d), lambda i, h: (i, 0)),
                  pl.BlockSpec((1, d), lambda i, h: (0, 0)),
                  pl.BlockSpec(wq_t.shape, lambda i, h: (0, 0)),
                  pl.BlockSpec((None, N_KEYS, N_KEYS), lambda i, h: (h, 0, 0)),
                  pl.BlockSpec((None, N_KEYS, N_KEYS), lambda i, h: (h, 0, 0))],
        out_specs=(pl.BlockSpec((d, tn), lambda i, h: (0, i)), side_spec, side_spec, side_spec, side_spec),
        scratch_shapes=[pltpu.VMEM((wq_t.shape[0], tn), F32)],
        compiler_params=_params("parallel", "arbitrary"),
        name="peer_route",
    )(x, g, wq_t, keys1, keys2)


def _peer_expert_kernel(x_ref, hnt_ref, u_ref, vt_prev_ref, vt_ref, r2_ref, cnt1_ref, a1_ref, w2_ref, o_ref,
                        acc_scr, pa_scr, pb_scr):
    j = pl.program_id(1)
    n_pair = pl.num_programs(1) - 1
    heads = r2_ref.shape[0]
    eb = u_ref.shape[0] // 2

    def build(p_scr, base):
        hn_t = hnt_ref[...]
        for c0 in range(0, eb, EXPERT_CHUNK):
            ht = jnp.dot(u_ref[base + c0:base + c0 + EXPERT_CHUNK, :], hn_t, preferred_element_type=F32)
            act = _gelu(ht).astype(BF16)
            gates = []
            for e in range((base + c0) // N_KEYS, (base + c0 + EXPERT_CHUNK) // N_KEYS):
                gate = None
                for h in range(heads):
                    c = cnt1_ref[h, e:e + 1, :].astype(BF16)
                    a = a1_ref[h, e:e + 1, :].astype(BF16)
                    wa = w2_ref[h] * a
                    term = jnp.where(r2_ref[h] < c, wa, jnp.zeros_like(wa))
                    gate = term if gate is None else gate + term
                gates.append(gate)
            p_scr[c0:c0 + EXPERT_CHUNK, :] = jnp.concatenate(gates, axis=0) * act

    @pl.when(j == 0)
    def _():
        acc_scr[...] = jnp.zeros_like(acc_scr)
        pb_scr[...] = jnp.zeros_like(pb_scr)

    @pl.when(j < n_pair)
    def _():
        acc_scr[...] += jnp.dot(vt_prev_ref[...], pb_scr[...], preferred_element_type=F32)
        build(pa_scr, 0)
        acc_scr[...] += jnp.dot(vt_ref[...], pa_scr[...], preferred_element_type=F32)
        build(pb_scr, eb)

    @pl.when(j == n_pair)
    def _():
        acc = acc_scr[...] + jnp.dot(vt_prev_ref[...], pb_scr[...], preferred_element_type=F32)
        o_ref[...] = x_ref[...] + acc.T


def peer_experts(x, hn_t, u, vt, r2, cnt1, a1, w2, tn=512, eb=1024):
    t, d = x.shape
    ne = u.shape[0]
    heads = r2.shape[0]
    tn = min(tn, t)
    n_pair = ne // (2 * eb)
    n_grp = 2 * eb // N_KEYS
    last = n_pair - 1
    full = pl.BlockSpec((heads, N_KEYS, tn), lambda i, j: (0, 0, i))
    part = pl.BlockSpec((heads, n_grp, tn), lambda i, j: (0, jnp.minimum(j, last), i))
    return pl.pallas_call(
        _peer_expert_kernel,
        out_shape=jax.ShapeDtypeStruct((t, d), F32),
        grid=(t // tn, n_pair + 1),
        in_specs=[pl.BlockSpec((tn, d), lambda i, j: (i, 0)),
                  pl.BlockSpec((d, tn), lambda i, j: (0, i)),
                  pl.BlockSpec((2 * eb, d), lambda i, j: (jnp.minimum(j, last), 0)),
                  pl.BlockSpec((d, eb), lambda i, j: (0, jnp.maximum(2 * j - 1, 0))),
                  pl.BlockSpec((d, eb), lambda i, j: (0, jnp.minimum(2 * j, 2 * last))),
                  full, part, part, full],
        out_specs=pl.BlockSpec((tn, d), lambda i, j: (i, 0)),
        scratch_shapes=[pltpu.VMEM((d, tn), F32), pltpu.VMEM((eb, tn), BF16), pltpu.VMEM((eb, tn), BF16)],
        compiler_params=_params("parallel", "arbitrary"),
        name="peer_experts",
    )(x, hn_t, u, vt, vt, r2, cnt1, a1, w2)


def peer_ffn_residual(x, g, w_query, keys1, keys2, u_table, v_table):
    wq_t = w_query.T.astype(BF16)
    hn_t, r2, cnt1, a1, w2 = peer_route(x, g, wq_t, keys1, keys2)
    return peer_experts(x, hn_t, u_table.astype(BF16), v_table.T.astype(BF16), r2, cnt1, a1, w2)


def kernel(x_prompt, x_sample, cache_k_sb, cache_v_sb, page_table, state_conv_sc, state_conv_rg, state_h_rg, norm_mix, norm_ffn, norm_final, w_qkv_sb, w_o_sb, b_sb, w_in_sc, conv_w_sc, w_out_sc, w_in_rg, conv_w_rg, conv_b_rg, w_ra_rg, b_ra_rg, w_ri_rg, b_ri_rg, lam_rg, w_out_rg, w_query_peer, keys1_peer, keys2_peer, u_peer, v_peer):
    bp, seq, d = x_prompt.shape
    bs, dseq, _ = x_sample.shape
    n_p = bp * seq
    depth = norm_mix.shape[0]
    n_pool = cache_k_sb.shape[1]
    n_pages = page_table.shape[1]
    x = jnp.concatenate([x_prompt.reshape(n_p, d), x_sample.reshape(bs * dseq, d)], axis=0)
    kcache = jnp.transpose(cache_k_sb, (0, 1, 3, 4, 2)).reshape(-1, N_HEADS, HEAD_DIM, PAGE)
    vcache = jnp.transpose(cache_v_sb, (0, 1, 3, 4, 2)).reshape(-1, N_HEADS, HEAD_DIM, PAGE)
    k_p, v_p, k_s, v_s = [], [], [], []
    csc_p, csc_s, crg_p, crg_s, h_p, h_s = [], [], [], [], [], []
    for layer in range(depth):
        kind, j = layer % 3, layer // 3
        g_mix = norm_mix[layer][None, :]
        if kind == 0:
            qkv = norm_matmul(x, g_mix, w_qkv_sb[j].astype(BF16))
            o_p = sb_prompt(qkv, b_sb[j], bp, seq)
            page_ids = (page_table + j * n_pool).reshape(-1)
            bias_rows = jnp.broadcast_to(jnp.repeat(b_sb[j], dseq)[:, None], (N_HEADS * dseq, PAGE))
            o_s = sb_sample(qkv, n_p, bs, dseq, page_ids, bias_rows, kcache, vcache)
            mix = (o_p, o_s)
            w_out = w_o_sb[j]
            k_p.append(qkv[:n_p, d:2 * d].reshape(bp, seq, N_HEADS, HEAD_DIM))
            v_p.append(qkv[:n_p, 2 * d:].reshape(bp, seq, N_HEADS, HEAD_DIM))
            k_s.append(qkv[n_p:, d:2 * d].reshape(bs, dseq, N_HEADS, HEAD_DIM))
            v_s.append(qkv[n_p:, 2 * d:].reshape(bs, dseq, N_HEADS, HEAD_DIM))
        elif kind == 1:
            proj = norm_matmul(x, g_mix, w_in_sc[j].astype(BF16))
            c = conv_w_sc.shape[2]
            g_p, nb_p = short_conv(proj, 0, bp, seq, conv_w_sc[j], jnp.zeros((bp, 2, c), F32))
            g_s, nb_s = short_conv(proj, n_p, bs, dseq, conv_w_sc[j], state_conv_sc[j])
            mix = (g_p, g_s)
            w_out = w_out_sc[j]
            csc_p.append(nb_p)
            csc_s.append(nb_s)
        else:
            proj = norm_matmul(x, g_mix, w_in_rg[j].astype(BF16))
            c = conv_w_rg.shape[2]
            wts = (conv_w_rg[j], conv_b_rg[j][None, :], w_ra_rg[j].astype(BF16), b_ra_rg[j][None, :],
                   w_ri_rg[j].astype(BF16), b_ri_rg[j][None, :], lam_rg[j][None, :])
            g_p, nb_p, hl_p = rglru(proj, 0, bp, seq, 2 * RG_BLOCK, *wts,
                                    jnp.zeros((bp, 3, c), F32), jnp.zeros((bp, 1, c), F32))
            g_s, nb_s, hl_s = rglru(proj, n_p, bs, dseq, c, *wts, state_conv_rg[j], state_h_rg[j][:, None, :])
            mix = (g_p, g_s)
            w_out = w_out_rg[j]
            crg_p.append(nb_p)
            crg_s.append(nb_s)
            h_p.append(hl_p[:, 0, :])
            h_s.append(hl_s[:, 0, :])
        x = matmul_residual(*mix, w_out.astype(BF16), x)
        x = peer_ffn_residual(x, norm_ffn[layer][None, :], w_query_peer[layer], keys1_peer[layer],
                              keys2_peer[layer], u_peer[layer], v_peer[layer])
    y = final_norm(x, norm_final[None, :])
    return (y[:n_p].reshape(bp, seq, d), y[n_p:].reshape(bs, dseq, d),
            jnp.stack(k_p), jnp.stack(v_p), jnp.stack(csc_p), jnp.stack(crg_p), jnp.stack(h_p),
            jnp.stack(k_s), jnp.stack(v_s), jnp.stack(csc_s), jnp.stack(crg_s), jnp.stack(h_s))
```

```python
import functools
import math

import jax
import jax.numpy as jnp
from jax import lax
from jax.experimental import pallas as pl
from jax.experimental.pallas import tpu as pltpu

F32 = jnp.float32
BF16 = jnp.bfloat16
EPS = 1e-6
HEAD_DIM = 64
N_HEADS = 16
PAGE = 128
K_TILE = 128
OFF_DIAG_UNROLL = 4
RG_BLOCK = 128
RG_C = 8.0
TOPK = 16
N_KEYS = 128
EXPERT_CHUNK = 128
LANES = 128
SUBLANES = 8
VMEM_LIMIT = 48 * 1024 * 1024

NT_DIMS = (((1,), (1,)), ((), ()))


def _params(*sem):
    return pltpu.CompilerParams(dimension_semantics=sem, vmem_limit_bytes=VMEM_LIMIT)


def _softplus(z):
    return jnp.maximum(z, 0.0) + jnp.log(1.0 + jnp.exp(-jnp.abs(z)))


def _sigmoid(z):
    return 1.0 / (1.0 + jnp.exp(-z))


def _gelu(x):
    return 0.5 * x * (1.0 + lax.erf(x * (1.0 / math.sqrt(2.0))))


def _rms(x, g):
    ms = jnp.mean(x * x, axis=-1, keepdims=True)
    return x * lax.rsqrt(ms + EPS) * g


def _split_bf16(x):
    hi = x.astype(BF16)
    lo = (x - hi.astype(F32)).astype(BF16)
    return hi, lo


def _dot_3pass(a, b):
    a_hi, a_lo = _split_bf16(a)
    b_hi, b_lo = _split_bf16(b)
    return (jnp.dot(a_hi, b_hi, preferred_element_type=F32)
            + (jnp.dot(a_hi, b_lo, preferred_element_type=F32) + jnp.dot(a_lo, b_hi, preferred_element_type=F32)))


def _norm_mm_kernel(x_ref, g_ref, w_ref, o_ref, hn_ref):
    @pl.when(pl.program_id(1) == 0)
    def _():
        hn_ref[...] = _rms(x_ref[...], g_ref[...]).astype(BF16)

    o_ref[...] = jnp.dot(hn_ref[...], w_ref[...], preferred_element_type=F32)


def norm_matmul(x, g, w, tm=512, tn=1024):
    t, d = x.shape
    n = w.shape[1]
    tm = min(tm, t)
    tn = min(tn, n)
    return pl.pallas_call(
        _norm_mm_kernel,
        out_shape=jax.ShapeDtypeStruct((t, n), F32),
        grid=(t // tm, n // tn),
        in_specs=[pl.BlockSpec((tm, d), lambda i, j: (i, 0)),
                  pl.BlockSpec((1, d), lambda i, j: (0, 0)),
                  pl.BlockSpec((d, tn), lambda i, j: (0, j))],
        out_specs=pl.BlockSpec((tm, tn), lambda i, j: (i, j)),
        scratch_shapes=[pltpu.VMEM((tm, d), BF16)],
        compiler_params=_params("parallel", "arbitrary"),
        name="norm_matmul",
    )(x, g, w)


def _mm_res_kernel(n_first, a1_ref, a2_ref, w_ref, r_ref, o_ref):
    i = pl.program_id(0)

    @pl.when(i < n_first)
    def _():
        o_ref[...] = r_ref[...] + jnp.dot(a1_ref[...].astype(BF16), w_ref[...], preferred_element_type=F32)

    @pl.when(i >= n_first)
    def _():
        o_ref[...] = r_ref[...] + jnp.dot(a2_ref[...].astype(BF16), w_ref[...], preferred_element_type=F32)


def matmul_residual(a1, a2, w, res, tm=512):
    t1, k = a1.shape
    t2 = a2.shape[0]
    n = w.shape[1]
    tm = math.gcd(tm, math.gcd(t1, t2))
    n1, n2 = t1 // tm, t2 // tm
    return pl.pallas_call(
        functools.partial(_mm_res_kernel, n1),
        out_shape=jax.ShapeDtypeStruct((t1 + t2, n), F32),
        grid=(n1 + n2,),
        in_specs=[pl.BlockSpec((tm, k), lambda i: (jnp.minimum(i, n1 - 1), 0)),
                  pl.BlockSpec((tm, k), lambda i: (jnp.maximum(i - n1, 0), 0)),
                  pl.BlockSpec((k, n), lambda i: (0, 0)),
                  pl.BlockSpec((tm, n), lambda i: (i, 0))],
        out_specs=pl.BlockSpec((tm, n), lambda i: (i, 0)),
        compiler_params=_params("parallel"),
        name="matmul_residual",
    )(a1, a2, w, res)


def _final_norm_kernel(x_ref, g_ref, o_ref):
    o_ref[...] = _rms(x_ref[...], g_ref[...])


def final_norm(x, g, tm=512):
    t, d = x.shape
    tm = min(tm, t)
    return pl.pallas_call(
        _final_norm_kernel,
        out_shape=jax.ShapeDtypeStruct((t, d), F32),
        grid=(t // tm,),
        in_specs=[pl.BlockSpec((tm, d), lambda i: (i, 0)), pl.BlockSpec((1, d), lambda i: (0, 0))],
        out_specs=pl.BlockSpec((tm, d), lambda i: (i, 0)),
        compiler_params=_params("parallel"),
        name="final_norm",
    )(x, g)


def _sb_prompt_kernel(bias_ref, q_ref, k_ref, v_ref, o_ref, acc_scr, cs_scr):
    hp = pl.program_id(1)
    qi = pl.program_id(2)
    tq = q_ref.shape[0]
    row = lax.broadcasted_iota(jnp.int32, (2 * tq, K_TILE), 0)
    col = lax.broadcasted_iota(jnp.int32, (2 * tq, K_TILE), 1)
    head1 = row >= tq
    q_pos = jnp.where(head1, row - tq, row) + qi * tq
    tri_r = lax.broadcasted_iota(jnp.int32, (2 * K_TILE, K_TILE), 0) & (K_TILE - 1)
    tri_c = lax.broadcasted_iota(jnp.int32, (2 * K_TILE, K_TILE), 1)
    suffix2 = jnp.where(tri_r >= tri_c, 1.0, 0.0).astype(BF16)
    first = lax.broadcasted_iota(jnp.int32, (tq, 2 * HEAD_DIM), 1) < HEAD_DIM
    q = q_ref[...] * (1.0 / math.sqrt(HEAD_DIM))
    q2 = jnp.concatenate([jnp.where(first, q, 0.0), jnp.where(first, 0.0, q)], axis=0).astype(BF16)
    bias = jnp.where(head1, bias_ref[hp * 2 + 1], bias_ref[hp * 2])
    n_diag = tq // K_TILE
    n_kb = (qi + 1) * n_diag

    def tile(kb, cs, masked):
        rows = pl.ds(pl.multiple_of(kb * K_TILE, K_TILE), K_TILE)
        k = k_ref[rows, :].astype(BF16)
        v = v_ref[rows, :].astype(BF16)
        z = lax.dot_general(q2, k, NT_DIMS, preferred_element_type=F32) + bias
        ln = -_softplus(z)
        if masked:
            mask = (col + kb * K_TILE) < q_pos
            ln = jnp.where(mask, ln, 0.0)
        hi, lo = _split_bf16(ln)
        part = jnp.dot(jnp.concatenate([hi, lo], axis=1), suffix2, preferred_element_type=F32)
        w = jnp.exp(z + (part + cs))
        if masked:
            w = jnp.where(mask, w, 0.0)
        return jnp.dot(w.astype(BF16), v, preferred_element_type=F32), cs + part[:, 0:1]

    cs = jnp.zeros((2 * tq, 1), F32)
    acc = None
    for t in range(n_diag):
        pv, cs = tile(n_kb - 1 - t, cs, True)
        acc = pv if acc is None else acc + pv
    acc_scr[...] = acc
    cs_scr[...] = cs

    def off_diagonal(kb, count):
        cs = cs_scr[...]
        total = None
        for t in range(count):
            pv, cs = tile(kb - t, cs, False)
            total = pv if total is None else total + pv
        acc_scr[...] += total
        cs_scr[...] = cs

    n_off = n_kb - n_diag

    def body(i, _):
        off_diagonal(n_off - 1 - OFF_DIAG_UNROLL * i, OFF_DIAG_UNROLL)
        return 0

    lax.fori_loop(0, n_off // OFF_DIAG_UNROLL, body, 0)
    rest = n_diag % OFF_DIAG_UNROLL
    if rest:
        @pl.when(n_off % OFF_DIAG_UNROLL != 0)
        def _():
            off_diagonal(rest - 1, rest)

    o_ref[...] = jnp.where(first, acc_scr[0:tq, :], acc_scr[tq:2 * tq, :])


def sb_prompt(qkv, bias, n_batch, seq, tq=256):
    d = qkv.shape[1] // 3
    n_pair = d // (2 * HEAD_DIM)
    tq = min(tq, seq)
    assert OFF_DIAG_UNROLL == 2 * (tq // K_TILE) or (tq // K_TILE) % OFF_DIAG_UNROLL == 0
    n_q = seq // tq
    return pl.pallas_call(
        _sb_prompt_kernel,
        out_shape=jax.ShapeDtypeStruct((n_batch * seq, d), F32),
        grid=(n_batch, n_pair, n_q),
        in_specs=[pl.BlockSpec(memory_space=pltpu.SMEM),
                  pl.BlockSpec((tq, 2 * HEAD_DIM), lambda b, p, i: (b * n_q + i, p)),
                  pl.BlockSpec((seq, 2 * HEAD_DIM), lambda b, p, i: (b, n_pair + p)),
                  pl.BlockSpec((seq, 2 * HEAD_DIM), lambda b, p, i: (b, 2 * n_pair + p))],
        out_specs=pl.BlockSpec((tq, 2 * HEAD_DIM), lambda b, p, i: (b * n_q + i, p)),
        scratch_shapes=[pltpu.VMEM((2 * tq, 2 * HEAD_DIM), F32), pltpu.VMEM((2 * tq, 1), F32)],
        compiler_params=_params("parallel", "parallel", "arbitrary"),
        name="sb_prompt",
    )(bias, qkv, qkv, qkv)


def _sb_sample_kernel(pt_ref, q_ref, kn_ref, vn_ref, bias_ref, *rest):
    del pt_ref
    pps = (len(rest) - 6) // 2
    kc_refs, vc_refs = rest[:pps], rest[pps:2 * pps]
    o_ref, qrep_ref, kpad_ref, vpad_ref, acc_ref, cs_ref = rest[2 * pps:]
    s = pl.program_id(1)
    n_q = q_ref.shape[0]
    d = q_ref.shape[1]
    row = lax.broadcasted_iota(jnp.int32, (LANES, PAGE), 0)
    lane = lax.broadcasted_iota(jnp.int32, (LANES, PAGE), 1)
    tri_r = lax.broadcasted_iota(jnp.int32, (2 * PAGE, PAGE), 0) & (PAGE - 1)
    tri_c = lax.broadcasted_iota(jnp.int32, (2 * PAGE, PAGE), 1)
    suffix2 = jnp.where(tri_r >= tri_c, 1.0, 0.0).astype(BF16)

    def process(kblks, vblks, mask, paged):
        zs, parts = [], []
        for kblk in kblks:
            if paged:
                z = jnp.dot(qrep_ref[...], kblk, preferred_element_type=F32)
            else:
                z = lax.dot_general(qrep_ref[...], kblk, NT_DIMS, preferred_element_type=F32)
            z = z + bias_ref[...]
            ln = -_softplus(z)
            if mask is not None:
                ln = jnp.where(mask, ln, 0.0)
            hi, lo = _split_bf16(ln)
            parts.append(jnp.dot(jnp.concatenate([hi, lo], axis=1), suffix2, preferred_element_type=F32))
            zs.append(z)
        carry = cs_ref[...]
        total = None
        for z, part, vblk in zip(zs, parts, vblks):
            w = jnp.exp(z + (part + carry))
            if mask is not None:
                w = jnp.where(mask, w, 0.0)
            if paged:
                pv = lax.dot_general(w.astype(BF16), vblk, NT_DIMS, preferred_element_type=F32)
            else:
                pv = jnp.dot(w.astype(BF16), vblk, preferred_element_type=F32)
            total = pv if total is None else total + pv
            carry = carry + part[:, 0:1]
        acc_ref[...] += total
        cs_ref[...] = carry

    @pl.when(s == 0)
    def _():
        q = q_ref[...] * (1.0 / math.sqrt(HEAD_DIM))
        qrep = jnp.concatenate([q] * N_HEADS, axis=0)
        r_head = lax.broadcasted_iota(jnp.int32, (N_HEADS * n_q, d), 0) >> (n_q.bit_length() - 1)
        c_head = lax.broadcasted_iota(jnp.int32, (N_HEADS * n_q, d), 1) >> (HEAD_DIM.bit_length() - 1)
        qrep_ref[...] = jnp.where(r_head == c_head, qrep, 0.0).astype(BF16)
        acc_ref[...] = jnp.zeros_like(acc_ref)
        cs_ref[...] = jnp.zeros_like(cs_ref)
        kpad_ref[...] = jnp.zeros_like(kpad_ref)
        vpad_ref[...] = jnp.zeros_like(vpad_ref)
        kpad_ref[0:n_q, :] = kn_ref[...]
        vpad_ref[0:n_q, :] = vn_ref[...]
        mask = lane < (row & (n_q - 1))
        process([kpad_ref[...].astype(BF16)], [vpad_ref[...].astype(BF16)], mask, paged=False)

    @pl.when(s > 0)
    def _():
        process([r[...].reshape(d, PAGE).astype(BF16) for r in kc_refs],
                [r[...].reshape(d, PAGE).astype(BF16) for r in vc_refs], None, paged=True)

    @pl.when(s == pl.num_programs(1) - 1)
    def _():
        half = lax.broadcasted_iota(jnp.int32, (n_q, LANES), 1) < HEAD_DIM
        outs = []
        for j in range(d // LANES):
            a = acc_ref[2 * j * n_q:(2 * j + 1) * n_q, j * LANES:(j + 1) * LANES]
            b = acc_ref[(2 * j + 1) * n_q:(2 * j + 2) * n_q, j * LANES:(j + 1) * LANES]
            outs.append(jnp.where(half, a, b))
        o_ref[...] = jnp.concatenate(outs, axis=-1)


def sb_sample(qkv, row0, n_batch, n_q, page_ids, bias_rows, kcache, vcache, pps=8):
    d = qkv.shape[1] // 3
    n_pages = page_ids.shape[0] // n_batch
    pps = math.gcd(pps, n_pages)
    assert N_HEADS * n_q == LANES and n_q == SUBLANES and row0 % n_q == 0
    rb = row0 // n_q

    def page_spec(r):
        def page_map(b, s, pt):
            return (pt[b * n_pages + n_pages - pps * jnp.maximum(s - 1, 0) - 1 - r], 0, 0, 0)
        return pl.BlockSpec((None, N_HEADS, HEAD_DIM, PAGE), page_map)

    grid_spec = pltpu.PrefetchScalarGridSpec(
        num_scalar_prefetch=1,
        grid=(n_batch, n_pages // pps + 1),
        in_specs=[pl.BlockSpec((n_q, d), lambda b, s, pt: (rb + b, 0)),
                  pl.BlockSpec((n_q, d), lambda b, s, pt: (rb + b, 1)),
                  pl.BlockSpec((n_q, d), lambda b, s, pt: (rb + b, 2)),
                  pl.BlockSpec((LANES, PAGE), lambda b, s, pt: (0, 0))]
                 + [page_spec(r) for r in range(pps)] + [page_spec(r) for r in range(pps)],
        out_specs=pl.BlockSpec((n_q, d), lambda b, s, pt: (b, 0)),
        scratch_shapes=[pltpu.VMEM((LANES, d), BF16),
                        pltpu.VMEM((PAGE, d), F32),
                        pltpu.VMEM((PAGE, d), F32),
                        pltpu.VMEM((LANES, d), F32),
                        pltpu.VMEM((LANES, 1), F32)],
    )
    return pl.pallas_call(
        _sb_sample_kernel,
        out_shape=jax.ShapeDtypeStruct((n_batch * n_q, d), F32),
        grid_spec=grid_spec,
        compiler_params=_params("parallel", "arbitrary"),
        name="sb_sample",
    )(page_ids, qkv, qkv, qkv, bias_rows, *([kcache] * pps), *([vcache] * pps))


def _shifted(u, k, buf_ref, row):
    nb = buf_ref.shape[0]
    s = pltpu.roll(u, k, axis=0)
    for r in range(k):
        s = jnp.where(row == r, buf_ref[nb - k + r:nb - k + r + 1, :], s)
    return s


def _short_conv_kernel(bg_ref, cg_ref, xv_ref, w_ref, buf_ref, g_ref, nb_ref):
    u = cg_ref[...] * xv_ref[...]
    t = u.shape[0]
    row = lax.broadcasted_iota(jnp.int32, u.shape, 0)
    y = _shifted(u, 2, buf_ref, row) * w_ref[0:1, :]
    y = y + _shifted(u, 1, buf_ref, row) * w_ref[1:2, :]
    y = y + u * w_ref[2:3, :]
    g_ref[...] = bg_ref[...] * y
    nb_ref[...] = u[t - 2:t, :]


def short_conv(proj, row0, n_batch, seq, conv_w, buf, tc=512):
    c = proj.shape[1] // 3
    nc = c // tc
    rb = row0 // seq
    return pl.pallas_call(
        _short_conv_kernel,
        out_shape=(jax.ShapeDtypeStruct((n_batch * seq, c), F32),
                   jax.ShapeDtypeStruct((n_batch, 2, c), F32)),
        grid=(n_batch, nc),
        in_specs=[pl.BlockSpec((seq, tc), lambda b, j: (rb + b, j)),
                  pl.BlockSpec((seq, tc), lambda b, j: (rb + b, nc + j)),
                  pl.BlockSpec((seq, tc), lambda b, j: (rb + b, 2 * nc + j)),
                  pl.BlockSpec((3, tc), lambda b, j: (0, j)),
                  pl.BlockSpec((None, 2, tc), lambda b, j: (b, 0, j))],
        out_specs=(pl.BlockSpec((seq, tc), lambda b, j: (b, j)),
                   pl.BlockSpec((None, 2, tc), lambda b, j: (b, 0, j))),
        compiler_params=_params("parallel", "parallel"),
        name="short_conv",
    )(proj, proj, proj, conv_w, buf)


def _scan_tile(a, d, row):
    for k in (1, 2, 4):
        a_s = pltpu.roll(a, k, axis=0)
        d_s = pltpu.roll(d, k, axis=0)
        valid = row >= k
        d = jnp.where(valid, a * d_s + d, d)
        a = jnp.where(valid, a * a_s, a)
    return a, d


def _rglru_kernel(gate_ref, u_ref, cw_ref, cb_ref, wra_ref, bra_ref, wri_ref, bri_ref, lam_ref,
                  buf_ref, h0_ref, o_ref, nb_ref, hl_ref, a_scr, d_scr):
    up = u_ref[...]
    t, c = up.shape
    row = lax.broadcasted_iota(jnp.int32, up.shape, 0)
    y = _shifted(up, 3, buf_ref, row) * cw_ref[0:1, :]
    y = y + _shifted(up, 2, buf_ref, row) * cw_ref[1:2, :]
    y = y + _shifted(up, 1, buf_ref, row) * cw_ref[2:3, :]
    y = y + up * cw_ref[3:4, :]
    u = y + cb_ref[...]
    rs, igs = [], []
    for n in range(c // RG_BLOCK):
        ub = u[:, n * RG_BLOCK:(n + 1) * RG_BLOCK].astype(BF16)
        rs.append(jnp.dot(ub, wra_ref[n], preferred_element_type=F32))
        igs.append(jnp.dot(ub, wri_ref[n], preferred_element_type=F32))
    r = _sigmoid(jnp.concatenate(rs, axis=-1) + bra_ref[...])
    ig = _sigmoid(jnp.concatenate(igs, axis=-1) + bri_ref[...])
    log_a = -RG_C * r * _softplus(-lam_ref[...])
    a = jnp.exp(log_a)
    drive = jnp.sqrt(-jnp.tanh(log_a) * (a * a + 1.0)) * (ig * u)
    a_scr[...] = a
    d_scr[...] = drive
    row8 = lax.broadcasted_iota(jnp.int32, (SUBLANES, c), 0)

    def body(i, h):
        rows = pl.ds(pl.multiple_of(i * SUBLANES, SUBLANES), SUBLANES)
        a_c, d_c = _scan_tile(a_scr[rows, :], d_scr[rows, :], row8)
        hs = a_c * h + d_c
        d_scr[rows, :] = hs
        return hs[SUBLANES - 1:SUBLANES, :]

    h_last = lax.fori_loop(0, t // SUBLANES, body, h0_ref[...])
    o_ref[...] = _gelu(gate_ref[...]) * d_scr[...]
    nb_ref[...] = up[t - 3:t, :]
    hl_ref[...] = h_last


def rglru(proj, row0, n_batch, seq, tc, conv_w, conv_b, w_ra, b_ra, w_ri, b_ri, lam, buf, h0):
    c = proj.shape[1] // 2
    nc = c // tc
    nblk = tc // RG_BLOCK
    rb = row0 // seq
    vec = pl.BlockSpec((1, tc), lambda b, j: (0, j))
    return pl.pallas_call(
        _rglru_kernel,
        out_shape=(jax.ShapeDtypeStruct((n_batch * seq, c), F32),
                   jax.ShapeDtypeStruct((n_batch, 3, c), F32),
                   jax.ShapeDtypeStruct((n_batch, 1, c), F32)),
        grid=(n_batch, nc),
        in_specs=[pl.BlockSpec((seq, tc), lambda b, j: (rb + b, j)),
                  pl.BlockSpec((seq, tc), lambda b, j: (rb + b, nc + j)),
                  pl.BlockSpec((4, tc), lambda b, j: (0, j)),
                  vec,
                  pl.BlockSpec((nblk, RG_BLOCK, RG_BLOCK), lambda b, j: (j, 0, 0)),
                  vec,
                  pl.BlockSpec((nblk, RG_BLOCK, RG_BLOCK), lambda b, j: (j, 0, 0)),
                  vec,
                  vec,
                  pl.BlockSpec((None, 3, tc), lambda b, j: (b, 0, j)),
                  pl.BlockSpec((None, 1, tc), lambda b, j: (b, 0, j))],
        out_specs=(pl.BlockSpec((seq, tc), lambda b, j: (b, j)),
                   pl.BlockSpec((None, 3, tc), lambda b, j: (b, 0, j)),
                   pl.BlockSpec((None, 1, tc), lambda b, j: (b, 0, j))),
        scratch_shapes=[pltpu.VMEM((seq, tc), F32), pltpu.VMEM((seq, tc), F32)],
        compiler_params=_params("parallel", "parallel"),
        name="rglru",
    )(proj, proj, conv_w, conv_b, w_ra, b_ra, w_ri, b_ri, lam, buf, h0)


def _top16(*scores):
    lanes = scores[0].shape[1]
    io_k = lax.broadcasted_iota(jnp.int32, (TOPK, lanes), 0)

    def body(i, carry):
        new = []
        for s, rank, vals in carry:
            n = s.shape[0]
            io = lax.broadcasted_iota(jnp.int32, s.shape, 0)
            m = jnp.max(s, axis=0, keepdims=True)
            idx = jnp.min(jnp.where(s == m, io, n), axis=0, keepdims=True)
            sel = io == idx
            new.append((jnp.where(sel, -jnp.inf, s), jnp.where(sel, i, rank), jnp.where(io_k == i, m, vals)))
        return tuple(new)

    init = tuple((s, jnp.full(s.shape, TOPK, jnp.int32), jnp.zeros((TOPK, lanes), F32)) for s in scores)
    return [(rank, vals) for _, rank, vals in lax.fori_loop(0, TOPK, body, init)]


def _top16_distinct(*scores):
    lanes = scores[0].shape[1]
    io_k = lax.broadcasted_iota(jnp.int32, (TOPK, lanes), 0)

    def body(i, carry):
        new = []
        for s, vals in carry:
            m = jnp.max(s, axis=0, keepdims=True)
            new.append((jnp.where(s == m, -jnp.inf, s), jnp.where(io_k == i, m, vals)))
        return tuple(new)

    init = tuple((s, jnp.zeros((TOPK, lanes), F32)) for s in scores)
    return list(lax.fori_loop(0, TOPK, body, init))


def _removed_count(s_after):
    return jnp.sum(jnp.where(s_after == -jnp.inf, 1.0, 0.0), axis=0, keepdims=True)


def _staircase(v1, v2):
    io8 = lax.broadcasted_iota(jnp.int32, (SUBLANES, v1.shape[1]), 0)
    pad = float(jnp.finfo(F32).min)
    parts = [v1[0:1, :] + v2, v1[1:2, :] + v2[0:SUBLANES, :]]
    for i in range(2, SUBLANES):
        parts.append(jnp.where(io8 < TOPK // (i + 1), v1[i:i + 1, :] + v2[0:SUBLANES, :], pad))
    parts.append(v1[SUBLANES:TOPK, :] + v2[0:1, :])
    return jnp.concatenate(parts, axis=0)


def _staircase_counts(sel):
    rows = [jnp.sum(sel[0:TOPK, :], axis=0, keepdims=True)]
    for i in range(1, SUBLANES):
        lo = TOPK + SUBLANES * (i - 1)
        rows.append(jnp.sum(sel[lo:lo + SUBLANES, :], axis=0, keepdims=True))
    lo = TOPK + SUBLANES * (SUBLANES - 1)
    rows += [sel[lo + r:lo + r + 1, :] for r in range(SUBLANES)]
    return rows


def _peer_route_kernel(x_ref, g_ref, wq_ref, k1_ref, k2_ref, hnt_ref, r2_ref, cnt1_ref, a1_ref, w2_ref, qt_scr):
    h = pl.program_id(1)
    tn = x_ref.shape[0]

    @pl.when(h == 0)
    def _():
        hn_t = _rms(x_ref[...], g_ref[...]).T.astype(BF16)
        hnt_ref[...] = hn_t
        qt_scr[...] = jnp.dot(wq_ref[...], hn_t, preferred_element_type=F32)

    off = pl.multiple_of(h * 2 * N_KEYS, 2 * N_KEYS)
    s1_all = _dot_3pass(k1_ref[...], qt_scr[pl.ds(off, N_KEYS), :])
    s2_all = _dot_3pass(k2_ref[...], qt_scr[pl.ds(off + N_KEYS, N_KEYS), :])
    groups = [slice(lg * LANES, (lg + 1) * LANES) for lg in range(tn // LANES)]

    def route(exact):
        halves = []
        for sl in groups:
            s1, s2 = s1_all[:, sl], s2_all[:, sl]
            if exact:
                (rank1, v1), (rank2, v2) = _top16(s1, s2)
                halves.append((s1, s2, v1, v2, rank1, rank2.astype(F32), None))
            else:
                (left1, v1), (left2, v2) = _top16_distinct(s1, s2)
                r2 = jnp.zeros((N_KEYS, LANES), F32)
                for i in range(TOPK):
                    r2 = r2 + jnp.where(v2[i:i + 1, :] > s2, 1.0, 0.0)
                r2 = jnp.where(left2 == -jnp.inf, r2, float(TOPK))
                off16 = jnp.abs(_removed_count(left1) - TOPK) + jnp.abs(_removed_count(left2) - TOPK)
                halves.append((s1, s2, v1, v2, None, r2, off16))
        cands = [_staircase(hv[2], hv[3]) for hv in halves]
        if exact:
            sels = [rank_c < TOPK for rank_c, _ in _top16(*cands)]
        else:
            lefts = [left_c for left_c, _ in _top16_distinct(*cands)]
            sels = [left_c == -jnp.inf for left_c in lefts]
        ties = None
        for g, sl in enumerate(groups):
            s1, s2, v1, v2, rank1, r2, off16 = halves[g]
            cand, sel = cands[g], sels[g]
            cnt = _staircase_counts(jnp.where(sel, 1.0, 0.0))
            z = jnp.sum(jnp.where(sel, jnp.exp(cand - cand[0:1, :]), 0.0), axis=0, keepdims=True)
            cnt1 = jnp.zeros((N_KEYS, LANES), F32)
            for i in range(TOPK):
                cnt1 = jnp.where((rank1 == i) if exact else (s1 == v1[i:i + 1, :]), cnt[i], cnt1)
            r2_ref[:, sl] = r2.astype(BF16)
            cnt1_ref[:, sl] = cnt1
            a1_ref[:, sl] = jnp.exp(s1 - v1[0:1, :])
            w2_ref[:, sl] = (jnp.exp(s2 - v2[0:1, :]) / z).astype(BF16)
            if not exact:
                off16 = off16 + jnp.abs(_removed_count(lefts[g]) - TOPK)
                ties = off16 if ties is None else ties + off16
        return ties

    ties = route(exact=False)

    @pl.when(jnp.max(ties) > 0.0)
    def _():
        route(exact=True)


def peer_route(x, g, wq_t, keys1, keys2, tn=512):
    t, d = x.shape
    heads = keys1.shape[0]
    tn = min(tn, t)
    side = jax.ShapeDtypeStruct((heads, N_KEYS, t), F32)
    side_bf = jax.ShapeDtypeStruct((heads, N_KEYS, t), BF16)
    side_spec = pl.BlockSpec((None, N_KEYS, tn), lambda i, h: (h, 0, i))
    return pl.pallas_call(
        _peer_route_kernel,
        out_shape=(jax.ShapeDtypeStruct((d, t), BF16), side_bf, side, side, side_bf),
        grid=(t // tn, heads),
        in_specs=[pl.BlockSpec((tn, d), lambda i, h: (i, 0)),
                  pl.BlockSpec((1, d), lambda i, h: (0, 0)),
                  pl.BlockSpec(wq_t.shape, lambda i, h: (0, 0)),
                  pl.BlockSpec((None, N_KEYS, N_KEYS), lambda i, h: (h, 0, 0)),
                  pl.BlockSpec((None, N_KEYS, N_KEYS), lambda i, h: (h, 0, 0))],
        out_specs=(pl.BlockSpec((d, tn), lambda i, h: (0, i)), side_spec, side_spec, side_spec, side_spec),
        scratch_shapes=[pltpu.VMEM((wq_t.shape[0], tn), F32)],
        compiler_params=_params("parallel", "arbitrary"),
        name="peer_route",
    )(x, g, wq_t, keys1, keys2)


def _peer_expert_kernel(x_ref, hnt_ref, u_ref, vt_prev_ref, vt_ref, r2_ref, cnt1_ref, a1_ref, w2_ref, o_ref,
                        acc_scr, pa_scr, pb_scr):
    j = pl.program_id(1)
    n_pair = pl.num_programs(1) - 1
    heads = r2_ref.shape[0]
    eb = u_ref.shape[0] // 2

    def build(p_scr, base):
        hn_t = hnt_ref[...]
        for c0 in range(0, eb, EXPERT_CHUNK):
            ht = jnp.dot(u_ref[base + c0:base + c0 + EXPERT_CHUNK, :], hn_t, preferred_element_type=F32)
            act = _gelu(ht).astype(BF16)
            gates = []
            for e in range((base + c0) // N_KEYS, (base + c0 + EXPERT_CHUNK) // N_KEYS):
                gate = None
                for h in range(heads):
                    c = cnt1_ref[h, e:e + 1, :].astype(BF16)
                    a = a1_ref[h, e:e + 1, :].astype(BF16)
                    wa = w2_ref[h] * a
                    term = jnp.where(r2_ref[h] < c, wa, jnp.zeros_like(wa))
                    gate = term if gate is None else gate + term
                gates.append(gate)
            p_scr[c0:c0 + EXPERT_CHUNK, :] = jnp.concatenate(gates, axis=0) * act

    @pl.when(j == 0)
    def _():
        acc_scr[...] = jnp.zeros_like(acc_scr)
        pb_scr[...] = jnp.zeros_like(pb_scr)

    @pl.when(j < n_pair)
    def _():
        acc_scr[...] += jnp.dot(vt_prev_ref[...], pb_scr[...], preferred_element_type=F32)
        build(pa_scr, 0)
        acc_scr[...] += jnp.dot(vt_ref[...], pa_scr[...], preferred_element_type=F32)
        build(pb_scr, eb)

    @pl.when(j == n_pair)
    def _():
        acc = acc_scr[...] + jnp.dot(vt_prev_ref[...], pb_scr[...], preferred_element_type=F32)
        o_ref[...] = x_ref[...] + acc.T


def peer_experts(x, hn_t, u, vt, r2, cnt1, a1, w2, tn=512, eb=1024):
    t, d = x.shape
    ne = u.shape[0]
    heads = r2.shape[0]
    tn = min(tn, t)
    n_pair = ne // (2 * eb)
    n_grp = 2 * eb // N_KEYS
    last = n_pair - 1
    full = pl.BlockSpec((heads, N_KEYS, tn), lambda i, j: (0, 0, i))
    part = pl.BlockSpec((heads, n_grp, tn), lambda i, j: (0, jnp.minimum(j, last), i))
    return pl.pallas_call(
        _peer_expert_kernel,
        out_shape=jax.ShapeDtypeStruct((t, d), F32),
        grid=(t // tn, n_pair + 1),
        in_specs=[pl.BlockSpec((tn, d), lambda i, j: (i, 0)),
                  pl.BlockSpec((d, tn), lambda i, j: (0, i)),
                  pl.BlockSpec((2 * eb, d), lambda i, j: (jnp.minimum(j, last), 0)),
                  pl.BlockSpec((d, eb), lambda i, j: (0, jnp.maximum(2 * j - 1, 0))),
                  pl.BlockSpec((d, eb), lambda i, j: (0, jnp.minimum(2 * j, 2 * last))),
                  full, part, part, full],
        out_specs=pl.BlockSpec((tn, d), lambda i, j: (i, 0)),
        scratch_shapes=[pltpu.VMEM((d, tn), F32), pltpu.VMEM((eb, tn), BF16), pltpu.VMEM((eb, tn), BF16)],
        compiler_params=_params("parallel", "arbitrary"),
        name="peer_experts",
    )(x, hn_t, u, vt, vt, r2, cnt1, a1, w2)


def peer_ffn_residual(x, g, w_query, keys1, keys2, u_table, v_table):
    wq_t = w_query.T.astype(BF16)
    hn_t, r2, cnt1, a1, w2 = peer_route(x, g, wq_t, keys1, keys2)
    return peer_experts(x, hn_t, u_table.astype(BF16), v_table.T.astype(BF16), r2, cnt1, a1, w2)


def kernel(x_prompt, x_sample, cache_k_sb, cache_v_sb, page_table, state_conv_sc, state_conv_rg, state_h_rg, norm_mix, norm_ffn, norm_final, w_qkv_sb, w_o_sb, b_sb, w_in_sc, conv_w_sc, w_out_sc, w_in_rg, conv_w_rg, conv_b_rg, w_ra_rg, b_ra_rg, w_ri_rg, b_ri_rg, lam_rg, w_out_rg, w_query_peer, keys1_peer, keys2_peer, u_peer, v_peer):
    bp, seq, d = x_prompt.shape
    bs, dseq, _ = x_sample.shape
    n_p = bp * seq
    depth = norm_mix.shape[0]
    n_pool = cache_k_sb.shape[1]
    n_pages = page_table.shape[1]
    x = jnp.concatenate([x_prompt.reshape(n_p, d), x_sample.reshape(bs * dseq, d)], axis=0)
    kcache = jnp.transpose(cache_k_sb, (0, 1, 3, 4, 2)).reshape(-1, N_HEADS, HEAD_DIM, PAGE)
    vcache = jnp.transpose(cache_v_sb, (0, 1, 3, 4, 2)).reshape(-1, N_HEADS, HEAD_DIM, PAGE)
    k_p, v_p, k_s, v_s = [], [], [], []
    csc_p, csc_s, crg_p, crg_s, h_p, h_s = [], [], [], [], [], []
    for layer in range(depth):
        kind, j = layer % 3, layer // 3
        g_mix = norm_mix[layer][None, :]
        if kind == 0:
            qkv = norm_matmul(x, g_mix, w_qkv_sb[j].astype(BF16))
            o_p = sb_prompt(qkv, b_sb[j], bp, seq)
            page_ids = (page_table + j * n_pool).reshape(-1)
            bias_rows = jnp.broadcast_to(jnp.repeat(b_sb[j], dseq)[:, None], (N_HEADS * dseq, PAGE))
            o_s = sb_sample(qkv, n_p, bs, dseq, page_ids, bias_rows, kcache, vcache)
            mix = (o_p, o_s)
            w_out = w_o_sb[j]
            k_p.append(qkv[:n_p, d:2 * d].reshape(bp, seq, N_HEADS, HEAD_DIM))
            v_p.append(qkv[:n_p, 2 * d:].reshape(bp, seq, N_HEADS, HEAD_DIM))
            k_s.append(qkv[n_p:, d:2 * d].reshape(bs, dseq, N_HEADS, HEAD_DIM))
            v_s.append(qkv[n_p:, 2 * d:].reshape(bs, dseq, N_HEADS, HEAD_DIM))
        elif kind == 1:
            proj = norm_matmul(x, g_mix, w_in_sc[j].astype(BF16))
            c = conv_w_sc.shape[2]
            g_p, nb_p = short_conv(proj, 0, bp, seq, conv_w_sc[j], jnp.zeros((bp, 2, c), F32))
            g_s, nb_s = short_conv(proj, n_p, bs, dseq, conv_w_sc[j], state_conv_sc[j])
            mix = (g_p, g_s)
            w_out = w_out_sc[j]
            csc_p.append(nb_p)
            csc_s.append(nb_s)
        else:
            proj = norm_matmul(x, g_mix, w_in_rg[j].astype(BF16))
            c = conv_w_rg.shape[2]
            wts = (conv_w_rg[j], conv_b_rg[j][None, :], w_ra_rg[j].astype(BF16), b_ra_rg[j][None, :],
                   w_ri_rg[j].astype(BF16), b_ri_rg[j][None, :], lam_rg[j][None, :])
            g_p, nb_p, hl_p = rglru(proj, 0, bp, seq, 2 * RG_BLOCK, *wts,
                                    jnp.zeros((bp, 3, c), F32), jnp.zeros((bp, 1, c), F32))
            g_s, nb_s, hl_s = rglru(proj, n_p, bs, dseq, c, *wts, state_conv_rg[j], state_h_rg[j][:, None, :])
            mix = (g_p, g_s)
            w_out = w_out_rg[j]
            crg_p.append(nb_p)
            crg_s.append(nb_s)
            h_p.append(hl_p[:, 0, :])
            h_s.append(hl_s[:, 0, :])
        x = matmul_residual(*mix, w_out.astype(BF16), x)
        x = peer_ffn_residual(x, norm_ffn[layer][None, :], w_query_peer[layer], keys1_peer[layer],
                              keys2_peer[layer], u_peer[layer], v_peer[layer])
    y = final_norm(x, norm_final[None, :])
    return (y[:n_p].reshape(bp, seq, d), y[n_p:].reshape(bs, dseq, d),
            jnp.stack(k_p), jnp.stack(v_p), jnp.stack(csc_p), jnp.stack(crg_p), jnp.stack(h_p),
            jnp.stack(k_s), jnp.stack(v_s), jnp.stack(csc_s), jnp.stack(crg_s), jnp.stack(h_s))
```
